```python
import jax, jax.numpy as jnp
from jax import lax
import numpy as np

D_MODEL = 1024
BATCH = 8
SEQ = 4096
DEPTH = 2

CHUNK = 64
N_HEADS_M = 4
DH_M = 256
W_M = N_HEADS_M * DH_M
CONV_W = 4
N_HEADS_A = 8
DH_A = 128
W_A = N_HEADS_A * DH_A
N_HEADS_IDX = 8
DH_IDX = 64
IDX_W_SCALE = (N_HEADS_IDX ** -0.5) * (DH_IDX ** -0.5)
TOPK_MAX = 256
Q_BLOCK = 128
D_FF = 2816
ROPE_THETA = 10000.0
EPS = 1e-6
NEG = -1e30
FORGET_BIAS_LO = 3.0
FORGET_BIAS_HI = 6.0
SPLITS = (W_M, W_M, W_M, W_M, N_HEADS_M, N_HEADS_M, W_A, DH_A, DH_A,
          N_HEADS_IDX * DH_IDX, N_HEADS_IDX, DH_IDX, D_MODEL, D_MODEL)
N_IN = sum(SPLITS)
OFF_F = 4 * W_M + N_HEADS_M

kernel_name = 'hybrid_mlstm_dsa_sandwich_block'


def rmsnorm(x, g):
    xf = x.astype(jnp.float32)
    y = xf * lax.rsqrt(jnp.mean(xf * xf, axis=-1, keepdims=True) + EPS)
    return (y * g.astype(jnp.float32)).astype(x.dtype)


def layernorm(x, g):
    xf = x.astype(jnp.float32)
    xc = xf - jnp.mean(xf, axis=-1, keepdims=True)
    y = xc * lax.rsqrt(jnp.mean(xc * xc, axis=-1, keepdims=True) + EPS)
    return (y * g.astype(jnp.float32)).astype(x.dtype)


def rope_tables(seq, dim):
    inv = ROPE_THETA ** (-jnp.arange(dim // 2, dtype=jnp.float32) / (dim // 2))
    ang = jnp.arange(seq, dtype=jnp.float32)[:, None] * inv[None, :]
    return jnp.cos(ang), jnp.sin(ang)


def apply_rope(x, cos, sin):
    half = x.shape[-1] // 2
    xf = x.astype(jnp.float32)
    x1, x2 = xf[..., :half], xf[..., half:]
    return jnp.concatenate([x1 * cos - x2 * sin, x2 * cos + x1 * sin], axis=-1).astype(x.dtype)


def causal_conv(x, w):
    s = x.shape[1]
    xp = jnp.pad(x, ((0, 0), (CONV_W - 1, 0), (0, 0)))
    out = xp[:, 0:s] * w[0]
    for j in range(1, CONV_W):
        out = out + xp[:, j:j + s] * w[j]
    return out


def mlstm_chunkwise(q, k, v, i_pre, f_pre):
    B, S, H, d = q.shape
    nc = S // CHUNK

    def heads_to_chunks(t):
        return t.astype(jnp.float32).reshape(B, nc, CHUNK, H, d).transpose(1, 0, 3, 2, 4)

    def gates_to_chunks(t):
        return t.astype(jnp.float32).reshape(B, nc, CHUNK, H).transpose(1, 0, 3, 2)

    qc = heads_to_chunks(q)
    kc = heads_to_chunks(k) * (d ** -0.5)
    vc = heads_to_chunks(v)
    ic = gates_to_chunks(i_pre)
    lfc = jax.nn.log_sigmoid(gates_to_chunks(f_pre))
    causal = jnp.tril(jnp.ones((CHUNK, CHUNK), dtype=bool))

    def step(carry, inp):
        C, n, m = carry
        qj, kj, vj, ij, lfj = inp
        b = jnp.cumsum(lfj, axis=-1)
        dlog = jnp.where(causal, b[..., :, None] - b[..., None, :] + ij[..., None, :], NEG)
        inter = b + m[..., None]
        mj = jnp.maximum(inter, jnp.max(dlog, axis=-1))
        dw = jnp.exp(dlog - mj[..., None])
        iw = jnp.exp(inter - mj)
        s = jnp.einsum('bhjd,bhsd->bhjs', qj, kj) * dw
        num = iw[..., None] * jnp.einsum('bhed,bhjd->bhje', C, qj) + jnp.einsum('bhjs,bhse->bhje', s, vj)
        den = iw * jnp.einsum('bhd,bhjd->bhj', n, qj) + jnp.sum(s, axis=-1)
        h = num / jnp.maximum(jnp.abs(den), jnp.exp(-mj))[..., None]
        b_last = b[..., -1]
        g = b_last[..., None] - b + ij
        m_new = jnp.maximum(b_last + m, jnp.max(g, axis=-1))
        decay = jnp.exp(b_last + m - m_new)
        w = jnp.exp(g - m_new[..., None])
        C_new = decay[..., None, None] * C + jnp.einsum('bhs,bhse,bhsd->bhed', w, vj, kj)
        n_new = decay[..., None] * n + jnp.einsum('bhs,bhsd->bhd', w, kj)
        return (C_new, n_new, m_new), h

    init = (jnp.zeros((B, H, d, d), jnp.float32), jnp.zeros((B, H, d), jnp.float32),
            jnp.zeros((B, H), jnp.float32))
    _, h = lax.scan(step, init, (qc, kc, vc, ic, lfc))
    return h.transpose(1, 0, 3, 2, 4).reshape(B, S, H, d)


def dsa_sparse_attention(q, k, v, qi, wi, ki):
    B, S, H, dh = q.shape
    n_sel = min(TOPK_MAX, S // 4)
    nb = S // Q_BLOCK
    key_chunk = jnp.arange(S, dtype=jnp.int32) // CHUNK
    kf = k.astype(jnp.float32)
    vf = v.astype(jnp.float32)
    kif = ki.astype(jnp.float32)

    def blocks(t):
        return jnp.moveaxis(t.astype(jnp.float32).reshape((B, nb, Q_BLOCK) + t.shape[2:]), 1, 0)

    def one_block(inp):
        qb, qib, wib, qch = inp
        rel = jax.nn.relu(jnp.einsum('bqhd,bsd->bqhs', qib, kif))
        score = jnp.einsum('bqh,bqhs->bqs', wib, rel)
        admissible = key_chunk[None, :] <= qch[:, None]
        score = jnp.where(admissible[None], score, NEG)
        _, idx = lax.top_k(score, n_sel)
        valid = (idx // CHUNK) <= qch[None, :, None]
        kg = jax.vmap(lambda kk, ii: kk[ii])(kf, idx)
        vg = jax.vmap(lambda vv, ii: vv[ii])(vf, idx)
        logits = jnp.einsum('bqhd,bqkd->bqhk', qb, kg) * (dh ** -0.5)
        logits = jnp.where(valid[:, :, None, :], logits, NEG)
        p = jax.nn.softmax(logits, axis=-1)
        return jnp.einsum('bqhk,bqkd->bqhd', p, vg)

    out = lax.map(one_block, (blocks(q), blocks(qi), blocks(wi), key_chunk.reshape(nb, Q_BLOCK)))
    return jnp.moveaxis(out, 0, 1).reshape(B, S, H * dh)


def setup_inputs(seed: int = 0) -> dict:
    key = jax.random.key(seed)
    ks = jax.random.split(key, 16)

    def nrm(k, shape, scale):
        return jax.random.normal(k, shape, jnp.float32) * scale

    def gain(k, shape):
        return 1.0 + 0.05 * jax.random.normal(k, shape, jnp.float32)

    b_in = nrm(ks[6], (DEPTH, N_IN), 0.02)
    b_in = b_in.at[:, OFF_F:OFF_F + N_HEADS_M].add(
        jnp.linspace(FORGET_BIAS_LO, FORGET_BIAS_HI, N_HEADS_M, dtype=jnp.float32))
    return {
        'x': nrm(ks[0], (BATCH, SEQ, D_MODEL), 1.0),
        'norm_mix_pre': gain(ks[1], (DEPTH, D_MODEL)),
        'norm_mix_post': gain(ks[2], (DEPTH, D_MODEL)),
        'norm_ffn_pre': gain(ks[3], (DEPTH, D_MODEL)),
        'norm_ffn_post': gain(ks[4], (DEPTH, D_MODEL)),
        'w_in': nrm(ks[5], (DEPTH, D_MODEL, N_IN), D_MODEL ** -0.5),
        'b_in': b_in,
        'conv_qk': nrm(ks[7], (DEPTH, CONV_W, 2 * W_M), CONV_W ** -0.5),
        'norm_mlstm_head': gain(ks[8], (DEPTH, W_M)),
        'norm_idx_k': gain(ks[9], (DEPTH, DH_IDX)),
        'w_branch_mlstm': nrm(ks[10], (DEPTH, W_M, D_MODEL), W_M ** -0.5),
        'w_branch_attn': nrm(ks[11], (DEPTH, W_A, D_MODEL), W_A ** -0.5),
        'w_out': nrm(ks[12], (DEPTH, D_MODEL, D_MODEL), D_MODEL ** -0.5),
        'w_ffn_gate': nrm(ks[13], (DEPTH, D_MODEL, D_FF), D_MODEL ** -0.5),
        'w_ffn_up': nrm(ks[14], (DEPTH, D_MODEL, D_FF), D_MODEL ** -0.5),
        'w_ffn_down': nrm(ks[15], (DEPTH, D_FF, D_MODEL), D_FF ** -0.5),
    }


def reference(x, norm_mix_pre, norm_mix_post, norm_ffn_pre, norm_ffn_post, w_in, b_in, conv_qk,
              norm_mlstm_head, norm_idx_k, w_branch_mlstm, w_branch_attn, w_out,
              w_ffn_gate, w_ffn_up, w_ffn_down):
    B, S = x.shape[0], x.shape[1]
    cos_a, sin_a = rope_tables(S, DH_A)
    cos_i, sin_i = rope_tables(S, DH_IDX)
    split_at = np.cumsum(SPLITS)[:-1].tolist()
    for l in range(DEPTH):
        h = rmsnorm(x, norm_mix_pre[l])
        proj = h @ w_in[l] + b_in[l]
        (mq, mk, mv, mo, mi, mf, aq, ak, av, iq, iw, ik, gm, ga) = jnp.split(proj, split_at, axis=-1)

        qk = jax.nn.silu(causal_conv(jnp.concatenate([mq, mk], axis=-1), conv_qk[l]))
        mq, mk = qk[..., :W_M], qk[..., W_M:]
        hm = mlstm_chunkwise(mq.reshape(B, S, N_HEADS_M, DH_M), mk.reshape(B, S, N_HEADS_M, DH_M),
                             mv.reshape(B, S, N_HEADS_M, DH_M), mi, mf)
        hm = rmsnorm(hm, norm_mlstm_head[l].reshape(N_HEADS_M, DH_M)).astype(x.dtype)
        hm = hm.reshape(B, S, W_M) * jax.nn.sigmoid(mo)

        qa = apply_rope(aq.reshape(B, S, N_HEADS_A, DH_A), cos_a[:, None, :], sin_a[:, None, :])
        ka = apply_rope(ak, cos_a, sin_a)
        qi = apply_rope(iq.reshape(B, S, N_HEADS_IDX, DH_IDX), cos_i[:, None, :], sin_i[:, None, :])
        ki = apply_rope(layernorm(ik, norm_idx_k[l]), cos_i, sin_i)
        ha = dsa_sparse_attention(qa, ka, av, qi, iw * IDX_W_SCALE, ki).astype(x.dtype)

        y = jax.nn.sigmoid(gm) * (hm @ w_branch_mlstm[l]) + jax.nn.sigmoid(ga) * (ha @ w_branch_attn[l])
        x = x + rmsnorm(y @ w_out[l], norm_mix_post[l])

        f = rmsnorm(x, norm_ffn_pre[l])
        f = (jax.nn.silu(f @ w_ffn_gate[l]) * (f @ w_ffn_up[l])) @ w_ffn_down[l]
        x = x + rmsnorm(f, norm_ffn_post[l])
    return x
```

```python
import functools

import jax
import jax.numpy as jnp
import numpy as np
from jax import lax
from jax.experimental import pallas as pl
from jax.experimental.pallas import tpu as pltpu

D_MODEL = 1024
CHUNK = 64
N_HEADS_M = 4
DH_M = 256
W_M = N_HEADS_M * DH_M
CONV_W = 4
N_HEADS_A = 8
DH_A = 128
W_A = N_HEADS_A * DH_A
N_HEADS_IDX = 8
DH_IDX = 64
W_IDX = N_HEADS_IDX * DH_IDX
IDX_W_SCALE = (N_HEADS_IDX ** -0.5) * (DH_IDX ** -0.5)
TOPK_MAX = 256
D_FF = 2816
ROPE_THETA = 10000.0
EPS = 1e-6
NEG = -1e30

LANES = 128
SUBLANES = 8
VMEM_LIMIT_BYTES = 56 * 1024 * 1024

OFF_MQ, OFF_MK, OFF_MV, OFF_MO = 0, 1024, 2048, 3072
OFF_AQ, OFF_GM, OFF_GA = 4096, 5120, 6144
OFF_IQ, OFF_AK, OFF_AV = 7168, 7680, 7808
N16 = 8192
C_IK, C_IW, C_MI, C_MF = 0, 64, 72, 76
N32 = 128

ML = 256
TQ = 256
TK = 256
INT_MIN = -2 ** 31

_F32 = jnp.float32
_BF16 = jnp.bfloat16


def _dot(a, b):
    return jnp.dot(a, b, preferred_element_type=_F32)


def _dot_nt(a, b):
    return lax.dot_general(a, b, (((1,), (1,)), ((), ())), preferred_element_type=_F32)


def _dot_tn(a, b):
    return lax.dot_general(a, b, (((0,), (0,)), ((), ())), preferred_element_type=_F32)


def _params(*sem):
    return pltpu.CompilerParams(dimension_semantics=sem, vmem_limit_bytes=VMEM_LIMIT_BYTES)


def _resident(shape):
    n = len(shape)
    return pl.BlockSpec(shape, lambda *_: (0,) * n, pipeline_mode=pl.Buffered(1))


def _in_proj_kernel(x_ref, g_ref, w_ref, b_ref, ws_ref, bs_ref, o16_ref, o32_ref, h_ref):
    @pl.when(pl.program_id(1) == 0)
    def _():
        x = x_ref[...]
        y = x * lax.rsqrt(jnp.mean(x * x, axis=-1, keepdims=True) + EPS)
        hb = (y * g_ref[...]).astype(_BF16)
        h_ref[...] = hb
        o32_ref[...] = _dot(hb, ws_ref[...]) + bs_ref[...]

    o16_ref[...] = (_dot(h_ref[...], w_ref[...]) + b_ref[...]).astype(_BF16)


def _in_proj(x2, g, w16, b16, w32, b32):
    n = x2.shape[0]
    tm, tn = 1024, 1024
    return pl.pallas_call(
        _in_proj_kernel,
        out_shape=(jax.ShapeDtypeStruct((n, N16), _BF16), jax.ShapeDtypeStruct((n, N32), _F32)),
        grid=(n // tm, N16 // tn),
        in_specs=[
            pl.BlockSpec((tm, D_MODEL), lambda i, j: (i, 0)),
            pl.BlockSpec((1, D_MODEL), lambda i, j: (0, 0)),
            pl.BlockSpec((D_MODEL, tn), lambda i, j: (0, j)),
            pl.BlockSpec((1, tn), lambda i, j: (0, j)),
            pl.BlockSpec((D_MODEL, N32), lambda i, j: (0, 0)),
            pl.BlockSpec((1, N32), lambda i, j: (0, 0)),
        ],
        out_specs=(
            pl.BlockSpec((tm, tn), lambda i, j: (i, j)),
            pl.BlockSpec((tm, N32), lambda i, j: (i, 0)),
        ),
        scratch_shapes=[pltpu.VMEM((tm, D_MODEL), _BF16)],
        compiler_params=_params("parallel", "arbitrary"),
        name="in_proj",
    )(x2, g, w16, b16, w32, b32)


def _log_sigmoid(x):
    return jnp.minimum(x, 0.0) - jnp.log(1.0 + jnp.exp(-jnp.abs(x)))


def _cumsum_rows(x):
    n = x.shape[0]
    row = lax.broadcasted_iota(jnp.int32, x.shape, 0)
    k = 1
    while k < n:
        x = x + jnp.where(row >= k, pltpu.roll(x, k, axis=0), 0.0)
        k *= 2
    return x


def _mlstm_kernel(qk_ref, v_ref, o_ref, g32_ref, cw_ref, hn_ref, out_ref,
                  ct_ref, n_ref, m_ref, tail_ref):
    @pl.when(pl.program_id(1) == 0)
    def _():
        ct_ref[...] = jnp.zeros_like(ct_ref)
        n_ref[...] = jnp.zeros_like(n_ref)
        m_ref[...] = jnp.zeros_like(m_ref)
        tail_ref[...] = jnp.zeros_like(tail_ref)

    L = ML
    qk_raw = qk_ref[...].astype(_F32)
    ext = jnp.concatenate([tail_ref[...], qk_raw], axis=0)
    tail_ref[...] = qk_raw[L - SUBLANES:, :]
    cw = cw_ref[...]
    conv = ext[SUBLANES:, :] * cw[CONV_W - 1:CONV_W, :]
    for j in range(1, CONV_W):
        conv = conv + pltpu.roll(ext, j, axis=0)[SUBLANES:, :] * cw[CONV_W - 1 - j:CONV_W - j, :]
    qk = conv * jax.nn.sigmoid(conv)

    g32 = g32_ref[...]
    bc = _cumsum_rows(_log_sigmoid(g32))
    g32_t = g32.T
    bc_t = bc.T
    row = lax.broadcasted_iota(jnp.int32, (L, L), 0)
    col = lax.broadcasted_iota(jnp.int32, (L, L), 1)
    causal = col <= row

    for h in range(N_HEADS_M):
        sl = slice(h * DH_M, (h + 1) * DH_M)
        q = qk[:, sl].astype(_BF16)
        k = (qk[:, W_M + h * DH_M:W_M + (h + 1) * DH_M] * (DH_M ** -0.5)).astype(_BF16)
        v = v_ref[:, sl]
        b_c = bc[:, C_MF + h:C_MF + h + 1]
        i_c = g32[:, C_MI + h:C_MI + h + 1]
        b_r = bc_t[C_MF + h:C_MF + h + 1, :]
        i_r = g32_t[C_MI + h:C_MI + h + 1, :]
        m_prev = m_ref[h]

        dlog = jnp.where(causal, b_c - b_r + i_r, NEG)
        inter = b_c + m_prev
        mj = jnp.maximum(inter, jnp.max(dlog, axis=-1, keepdims=True))
        dw = jnp.exp(dlog - mj)
        iw = jnp.exp(inter - mj)
        s = _dot_nt(q, k) * dw
        num = iw * _dot(q, ct_ref[h].astype(_BF16)) + _dot(s.astype(_BF16), v)
        qn = jnp.sum(q.astype(_F32) * n_ref[h], axis=-1, keepdims=True)
        den = iw * qn + jnp.sum(s, axis=-1, keepdims=True)
        hh = num / jnp.maximum(jnp.abs(den), jnp.exp(-mj))

        b_last = b_c[L - 1:L, :]
        gg = b_last - b_c + i_c
        m_new = jnp.maximum(b_last + m_prev, jnp.max(gg, axis=0, keepdims=True))
        decay = jnp.exp(b_last + m_prev - m_new)
        w = jnp.exp(gg - m_new)
        kf = k.astype(_F32)
        wv = (w * v.astype(_F32)).astype(_BF16)
        ct_ref[h] = decay * ct_ref[h] + _dot_tn(k, wv)
        n_ref[h] = decay * n_ref[h] + jnp.sum(w * kf, axis=0, keepdims=True)
        m_ref[h] = m_new

        y = hh * lax.rsqrt(jnp.mean(hh * hh, axis=-1, keepdims=True) + EPS) * hn_ref[:, sl]
        gate = jax.nn.sigmoid(o_ref[:, sl].astype(_F32))
        out_ref[:, sl] = (y * gate).astype(_BF16)


def _mlstm(p16, p32, conv_qk, norm_head, B, S):
    n = B * S
    nc = S // ML
    row = lambda b, c: b * nc + c
    return pl.pallas_call(
        _mlstm_kernel,
        out_shape=jax.ShapeDtypeStruct((n, W_M), _BF16),
        grid=(B, nc),
        in_specs=[
            pl.BlockSpec((ML, 2 * W_M), lambda b, c: (row(b, c), 0)),
            pl.BlockSpec((ML, W_M), lambda b, c: (row(b, c), OFF_MV // W_M)),
            pl.BlockSpec((ML, W_M), lambda b, c: (row(b, c), OFF_MO // W_M)),
            pl.BlockSpec((ML, N32), lambda b, c: (row(b, c), 0)),
            pl.BlockSpec((CONV_W, 2 * W_M), lambda b, c: (0, 0)),
            pl.BlockSpec((1, W_M), lambda b, c: (0, 0)),
        ],
        out_specs=pl.BlockSpec((ML, W_M), lambda b, c: (row(b, c), 0)),
        scratch_shapes=[
            pltpu.VMEM((N_HEADS_M, DH_M, DH_M), _F32),
            pltpu.VMEM((N_HEADS_M, 1, DH_M), _F32),
            pltpu.VMEM((N_HEADS_M, 1, 1), _F32),
            pltpu.VMEM((SUBLANES, 2 * W_M), _F32),
        ],
        compiler_params=_params("parallel", "arbitrary"),
        name="mlstm",
    )(p16, p16, p16, p32, conv_qk, norm_head)


def _rot_rows(x, half):
    return jnp.concatenate([x[half:], x[:half]], axis=0)


def _dsa_prep_kernel(aq_ref, iq_ref, ak_ref, av_ref, g32_ref, gk_ref,
                     cat_ref, sat_ref, cit_ref, sit_ref, ca_ref, sa_ref, ci_ref, si_ref,
                     qat_ref, qit_ref, wit_ref, ka_ref, ki_ref, vt_ref):
    cat, sat = cat_ref[...], sat_ref[...]
    aq_t = aq_ref[...].astype(_F32).T
    for h in range(N_HEADS_A):
        x = aq_t[h * DH_A:(h + 1) * DH_A]
        qat_ref[h * DH_A:(h + 1) * DH_A, :] = (x * cat + _rot_rows(x, DH_A // 2) * sat).astype(_BF16)

    cit, sit = cit_ref[...], sit_ref[...]
    iq_t = iq_ref[...].astype(_F32).T
    for h in range(N_HEADS_IDX):
        x = iq_t[h * DH_IDX:(h + 1) * DH_IDX]
        qit_ref[h * DH_IDX:(h + 1) * DH_IDX, :] = (x * cit + _rot_rows(x, DH_IDX // 2) * sit).astype(_BF16)

    g32 = g32_ref[...]
    wit_ref[...] = g32.T[C_IW:C_IW + N_HEADS_IDX, :] * IDX_W_SCALE

    ak = ak_ref[...].astype(_F32)
    ka_ref[...] = (ak * ca_ref[...] + pltpu.roll(ak, DH_A // 2, axis=1) * sa_ref[...]).astype(_BF16)

    ik = g32[:, C_IK:C_IK + DH_IDX]
    xc = ik - jnp.mean(ik, axis=-1, keepdims=True)
    ln = xc * lax.rsqrt(jnp.mean(xc * xc, axis=-1, keepdims=True) + EPS) * gk_ref[...]
    half = DH_IDX // 2
    ln_rot = jnp.concatenate([ln[:, half:], ln[:, :half]], axis=1)
    ki_ref[...] = (ln * ci_ref[...] + ln_rot * si_ref[...]).astype(_BF16)

    vt_ref[...] = av_ref[...].astype(_F32).T.astype(_BF16)


def _dsa_prep(p16, p32, norm_idx_k, tabs, B, S):
    nt = S // TQ
    row = lambda b, t: b * nt + t
    cat, sat, cit, sit, ca, sa, ci, si = tabs
    return pl.pallas_call(
        _dsa_prep_kernel,
        out_shape=(
            jax.ShapeDtypeStruct((B, W_A, S), _BF16),
            jax.ShapeDtypeStruct((B, W_IDX, S), _BF16),
            jax.ShapeDtypeStruct((B, N_HEADS_IDX, S), _F32),
            jax.ShapeDtypeStruct((B, S, DH_A), _BF16),
            jax.ShapeDtypeStruct((B, S, DH_IDX), _BF16),
            jax.ShapeDtypeStruct((B, DH_A, S), _BF16),
        ),
        grid=(B, nt),
        in_specs=[
            pl.BlockSpec((TQ, W_A), lambda b, t: (row(b, t), OFF_AQ // W_A)),
            pl.BlockSpec((TQ, W_IDX), lambda b, t: (row(b, t), OFF_IQ // W_IDX)),
            pl.BlockSpec((TQ, DH_A), lambda b, t: (row(b, t), OFF_AK // DH_A)),
            pl.BlockSpec((TQ, DH_A), lambda b, t: (row(b, t), OFF_AV // DH_A)),
            pl.BlockSpec((TQ, N32), lambda b, t: (row(b, t), 0)),
            pl.BlockSpec((1, DH_IDX), lambda b, t: (0, 0)),
            pl.BlockSpec((DH_A, TQ), lambda b, t: (0, t)),
            pl.BlockSpec((DH_A, TQ), lambda b, t: (0, t)),
            pl.BlockSpec((DH_IDX, TQ), lambda b, t: (0, t)),
            pl.BlockSpec((DH_IDX, TQ), lambda b, t: (0, t)),
            pl.BlockSpec((TQ, DH_A), lambda b, t: (t, 0)),
            pl.BlockSpec((TQ, DH_A), lambda b, t: (t, 0)),
            pl.BlockSpec((TQ, DH_IDX), lambda b, t: (t, 0)),
            pl.BlockSpec((TQ, DH_IDX), lambda b, t: (t, 0)),
        ],
        out_specs=(
            pl.BlockSpec((None, W_A, TQ), lambda b, t: (b, 0, t)),
            pl.BlockSpec((None, W_IDX, TQ), lambda b, t: (b, 0, t)),
            pl.BlockSpec((None, N_HEADS_IDX, TQ), lambda b, t: (b, 0, t)),
            pl.BlockSpec((None, TQ, DH_A), lambda b, t: (b, t, 0)),
            pl.BlockSpec((None, TQ, DH_IDX), lambda b, t: (b, t, 0)),
            pl.BlockSpec((None, DH_A, TQ), lambda b, t: (b, 0, t)),
        ),
        compiler_params=_params("parallel", "parallel"),
        name="dsa_prep",
    )(p16, p16, p16, p16, p32, norm_idx_k, cat, sat, cit, sit, ca, sa, ci, si)


def _allsum_sublanes(x):
    for k in (4, 2, 1):
        x = x + pltpu.roll(x, k, axis=0)
    return x


def _dsa_kernel(qit_ref, wit_ref, qat_ref, ki_ref, ka_ref, vt_ref, out_ref,
                keys_ref, acc_ref, m_ref, l_ref, jmax_ref, *, n_sel, idx_bits):
    t = pl.program_id(1)
    nkt = t + 1
    n_keys = nkt * TK
    grp = TK // SUBLANES
    q_chunk = (t * TQ + lax.broadcasted_iota(jnp.int32, (TK, TQ), 1)) // CHUNK
    row_in_tile = lax.broadcasted_iota(jnp.int32, (TK, TQ), 0)

    def score_body(kt, carry):
        r0 = pl.multiple_of(kt * TK, TK)
        ki = ki_ref[pl.ds(r0, TK), :]
        sc = jnp.zeros((TK, TQ), _F32)
        for h in range(N_HEADS_IDX):
            r = _dot(ki, qit_ref[h * DH_IDX:(h + 1) * DH_IDX, :])
            sc = sc + wit_ref[h:h + 1, :] * jnp.maximum(r, 0.0)
        k_chunk = (r0 + row_in_tile) // CHUNK
        sc = jnp.where(k_chunk <= q_chunk, sc, NEG)
        sc = jnp.where(sc == 0.0, 0.0, sc)
        bits = pltpu.bitcast(sc, jnp.int32)
        keys_ref[pl.ds(r0, TK), :] = bits ^ ((bits >> 31) & 0x7FFFFFFF)
        return carry

    lax.fori_loop(0, nkt, score_body, 0)

    def count(pred):
        def body(kt, acc):
            r0 = pl.multiple_of(kt * TK, TK)
            blk = keys_ref[pl.ds(r0, TK), :]
            for g in range(grp):
                acc = acc + pred(blk[g * SUBLANES:(g + 1) * SUBLANES], r0 + g * SUBLANES)
            return acc
        return _allsum_sublanes(lax.fori_loop(0, nkt, body, jnp.zeros((SUBLANES, TQ), jnp.int32)))

    def count_ge(cand):
        return count(lambda blk, r: jnp.where(blk >= cand, 1, 0))

    c0 = count_ge(jnp.zeros((SUBLANES, TQ), jnp.int32))
    ok0 = c0 >= n_sel
    thr0 = jnp.where(ok0, 0, INT_MIN)
    cge0 = jnp.where(ok0, c0, n_keys)

    def bit_step(i, carry):
        thr, cge = carry
        cand = thr + lax.shift_left(jnp.int32(1), 30 - i)
        c = count_ge(cand)
        ok = c >= n_sel
        return jnp.where(ok, cand, thr), jnp.where(ok, c, cge)

    thr, cge = lax.fori_loop(0, 31, bit_step, (thr0, cge0))

    sub = lax.broadcasted_iota(jnp.int32, (SUBLANES, TQ), 0)
    m_ref[...] = jnp.full_like(m_ref, NEG)
    l_ref[...] = jnp.zeros_like(l_ref)
    acc_ref[...] = jnp.zeros_like(acc_ref)
    jmax_ref[...] = jnp.full_like(jmax_ref, 2 ** 30)

    @pl.when(jnp.max(cge) > n_sel)
    def _():
        need = n_sel - count(lambda blk, r: jnp.where(blk > thr, 1, 0))

        def step(i, j):
            cand = j + lax.shift_left(jnp.int32(1), idx_bits - 1 - i)
            c = count(lambda blk, r: jnp.where(blk == thr, jnp.where(r + sub < cand, 1, 0), 0))
            return jnp.where(c < need, cand, j)

        jmax_ref[...] = lax.fori_loop(0, idx_bits, step, jnp.zeros((SUBLANES, TQ), jnp.int32))

    thr_row = thr[0:1, :]
    jmax_row = jmax_ref[0:1, :]

    scale = DH_A ** -0.5

    def attn_body(kt, carry):
        r0 = pl.multiple_of(kt * TK, TK)
        blk = keys_ref[pl.ds(r0, TK), :]
        krow = r0 + row_in_tile
        picked = jnp.where(blk > thr_row, 1, jnp.where(blk == thr_row, jnp.where(krow <= jmax_row, 1, 0), 0))
        sel = jnp.where(krow // CHUNK <= q_chunk, picked, 0) > 0
        ka = ka_ref[pl.ds(r0, TK), :]
        vt = vt_ref[:, pl.ds(r0, TK)]
        for h in range(N_HEADS_A):
            lg = _dot(ka, qat_ref[h * DH_A:(h + 1) * DH_A, :]) * scale
            lgm = jnp.where(sel, lg, NEG)
            m_old = m_ref[h:h + 1, :]
            m_new = jnp.maximum(m_old, jnp.max(lgm, axis=0, keepdims=True))
            alpha = jnp.exp(m_old - m_new)
            p = jnp.where(sel, jnp.exp(lgm - m_new), 0.0)
            l_ref[h:h + 1, :] = alpha * l_ref[h:h + 1, :] + jnp.sum(p, axis=0, keepdims=True)
            acc_ref[h] = alpha * acc_ref[h] + _dot(vt, p.astype(_BF16))
            m_ref[h:h + 1, :] = m_new
        return carry

    lax.fori_loop(0, nkt, attn_body, 0)

    for h in range(N_HEADS_A):
        o = acc_ref[h] / l_ref[h:h + 1, :]
        out_ref[:, h * DH_A:(h + 1) * DH_A] = o.T.astype(_BF16)


def _dsa(qat, qit, wit, ka, ki, vt, B, S):
    nt = S // TQ
    n_sel = min(TOPK_MAX, S // 4)
    idx_bits = max(1, int(np.ceil(np.log2(S))))
    kern = functools.partial(_dsa_kernel, n_sel=n_sel, idx_bits=idx_bits)
    return pl.pallas_call(
        kern,
        out_shape=jax.ShapeDtypeStruct((B * S, W_A), _BF16),
        grid=(B, nt),
        in_specs=[
            pl.BlockSpec((None, W_IDX, TQ), lambda b, t: (b, 0, t)),
            pl.BlockSpec((None, N_HEADS_IDX, TQ), lambda b, t: (b, 0, t)),
            pl.BlockSpec((None, W_A, TQ), lambda b, t: (b, 0, t)),
            pl.BlockSpec((None, S, DH_IDX), lambda b, t: (b, 0, 0)),
            pl.BlockSpec((None, S, DH_A), lambda b, t: (b, 0, 0)),
            pl.BlockSpec((None, DH_A, S), lambda b, t: (b, 0, 0)),
        ],
        out_specs=pl.BlockSpec((TQ, W_A), lambda b, t: (b * nt + t, 0)),
        scratch_shapes=[
            pltpu.VMEM((S, TQ), jnp.int32),
            pltpu.VMEM((N_HEADS_A, DH_A, TQ), _F32),
            pltpu.VMEM((N_HEADS_A, TQ), _F32),
            pltpu.VMEM((N_HEADS_A, TQ), _F32),
            pltpu.VMEM((SUBLANES, TQ), jnp.int32),
        ],
        compiler_params=_params("parallel", "arbitrary"),
        name="dsa",
    )(qit, wit, qat, ki, ka, vt)


def _merge_kernel(hm_ref, ha_ref, gm_ref, ga_ref, x_ref, wbm_ref, wba_ref, wo_ref, g_ref, out_ref):
    ym = _dot(hm_ref[...], wbm_ref[...])
    ya = _dot(ha_ref[...], wba_ref[...])
    y = jax.nn.sigmoid(gm_ref[...].astype(_F32)) * ym + jax.nn.sigmoid(ga_ref[...].astype(_F32)) * ya
    z = _dot(y.astype(_BF16), wo_ref[...])
    zn = z * lax.rsqrt(jnp.mean(z * z, axis=-1, keepdims=True) + EPS) * g_ref[...]
    out_ref[...] = x_ref[...] + zn


def _merge(hm, ha, p16, x2, wbm, wba, wo, g):
    n = x2.shape[0]
    tm = 512
    rows = lambda c: pl.BlockSpec((tm, D_MODEL), lambda i: (i, c))
    return pl.pallas_call(
        _merge_kernel,
        out_shape=jax.ShapeDtypeStruct((n, D_MODEL), _F32),
        grid=(n // tm,),
        in_specs=[rows(0), rows(0), rows(OFF_GM // D_MODEL), rows(OFF_GA // D_MODEL), rows(0),
                  _resident((W_M, D_MODEL)), _resident((W_A, D_MODEL)), _resident((D_MODEL, D_MODEL)),
                  _resident((1, D_MODEL))],
        out_specs=rows(0),
        compiler_params=_params("parallel"),
        name="merge",
    )(hm, ha, p16, p16, x2, wbm, wba, wo, g)


def _ffn_kernel(x_ref, gpre_ref, wg_ref, wu_ref, wd_ref, gpost_ref, out_ref):
    x = x_ref[...]
    f = (x * lax.rsqrt(jnp.mean(x * x, axis=-1, keepdims=True) + EPS) * gpre_ref[...]).astype(_BF16)
    a = _dot(f, wg_ref[...])
    u = _dot(f, wu_ref[...])
    act = (a * jax.nn.sigmoid(a) * u).astype(_BF16)
    z = _dot(act, wd_ref[...])
    out_ref[...] = x + z * lax.rsqrt(jnp.mean(z * z, axis=-1, keepdims=True) + EPS) * gpost_ref[...]


def _ffn(x2, gpre, wg, wu, wd, gpost):
    n = x2.shape[0]
    tm = 512
    rows = pl.BlockSpec((tm, D_MODEL), lambda i: (i, 0))
    return pl.pallas_call(
        _ffn_kernel,
        out_shape=jax.ShapeDtypeStruct((n, D_MODEL), _F32),
        grid=(n // tm,),
        in_specs=[rows, _resident((1, D_MODEL)), _resident((D_MODEL, D_FF)), _resident((D_MODEL, D_FF)),
                  _resident((D_FF, D_MODEL)), _resident((1, D_MODEL))],
        out_specs=rows,
        compiler_params=_params("parallel"),
        name="ffn",
    )(x2, gpre, wg, wu, wd, gpost)


def _pack_in_proj(w, b):
    o = np.cumsum((0, W_M, W_M, W_M, W_M, N_HEADS_M, N_HEADS_M, W_A, DH_A, DH_A,
                   W_IDX, N_HEADS_IDX, DH_IDX, D_MODEL, D_MODEL))
    mq, mk, mv, mo, mi, mf, aq, ak, av, iq, iw, ik, gm, ga = [slice(o[i], o[i + 1]) for i in range(14)]
    order16 = (mq, mk, mv, mo, aq, gm, ga, iq, ak, av)
    order32 = (ik, iw, mi, mf)

    def cat(arr, order, width):
        parts = [arr[..., s] for s in order]
        used = sum(p.shape[-1] for p in parts)
        parts.append(jnp.zeros(arr.shape[:-1] + (width - used,), arr.dtype))
        return jnp.concatenate(parts, axis=-1)

    return (cat(w, order16, N16).astype(_BF16), cat(b, order16, N16)[None, :],
            cat(w, order32, N32).astype(_BF16), cat(b, order32, N32)[None, :])


def _rope_tables(S):
    def tab(dim):
        inv = ROPE_THETA ** (-jnp.arange(dim // 2, dtype=_F32) / (dim // 2))
        ang = jnp.arange(S, dtype=_F32)[:, None] * inv[None, :]
        c, s = jnp.cos(ang), jnp.sin(ang)
        return jnp.concatenate([c, c], axis=-1), jnp.concatenate([-s, s], axis=-1)

    ca, sa = tab(DH_A)
    ci, si = tab(DH_IDX)
    return (ca.T, sa.T, ci.T, si.T, ca, sa, ci, si)


def kernel(x, norm_mix_pre, norm_mix_post, norm_ffn_pre, norm_ffn_post, w_in, b_in, conv_qk,
           norm_mlstm_head, norm_idx_k, w_branch_mlstm, w_branch_attn, w_out,
           w_ffn_gate, w_ffn_up, w_ffn_down):
    B, S, _ = x.shape
    assert S % ML == 0 and S % TQ == 0 and TQ == TK and (B * S) % 1024 == 0
    depth = w_in.shape[0]
    tabs = _rope_tables(S)
    x2 = x.reshape(B * S, D_MODEL)
    for l in range(depth):
        w16, b16, w32, b32 = _pack_in_proj(w_in[l], b_in[l])
        p16, p32 = _in_proj(x2, norm_mix_pre[l][None, :], w16, b16, w32, b32)
        hm = _mlstm(p16, p32, conv_qk[l], norm_mlstm_head[l][None, :], B, S)
        qat, qit, wit, ka, ki, vt = _dsa_prep(p16, p32, norm_idx_k[l][None, :], tabs, B, S)
        ha = _dsa(qat, qit, wit, ka, ki, vt, B, S)
        x2 = _merge(hm, ha, p16, x2, w_branch_mlstm[l].astype(_BF16), w_branch_attn[l].astype(_BF16),
                    w_out[l].astype(_BF16), norm_mix_post[l][None, :])
        x2 = _ffn(x2, norm_ffn_pre[l][None, :], w_ffn_gate[l].astype(_BF16), w_ffn_up[l].astype(_BF16),
                  w_ffn_down[l].astype(_BF16), norm_ffn_post[l][None, :])
    return x2.reshape(B, S, D_MODEL)
```

```python
import functools

import jax
import jax.numpy as jnp
import numpy as np
from jax import lax
from jax.experimental import pallas as pl
from jax.experimental.pallas import tpu as pltpu

D_MODEL = 1024
CHUNK = 64
N_HEADS_M = 4
DH_M = 256
W_M = N_HEADS_M * DH_M
CONV_W = 4
N_HEADS_A = 8
DH_A = 128
W_A = N_HEADS_A * DH_A
N_HEADS_IDX = 8
DH_IDX = 64
W_IDX = N_HEADS_IDX * DH_IDX
IDX_W_SCALE = (N_HEADS_IDX ** -0.5) * (DH_IDX ** -0.5)
TOPK_MAX = 256
D_FF = 2816
ROPE_THETA = 10000.0
EPS = 1e-6
NEG = -1e30

LANES = 128
SUBLANES = 8
VMEM_LIMIT_BYTES = 56 * 1024 * 1024

OFF_MQ, OFF_MK, OFF_MV, OFF_MO = 0, 1024, 2048, 3072
OFF_AQ, OFF_GM, OFF_GA = 4096, 5120, 6144
OFF_IQ, OFF_AK, OFF_AV = 7168, 7680, 7808
N16 = 8192
C_IK, C_IW, C_MI, C_MF = 0, 64, 72, 76
N32 = 128

ML = 256
TQ = 256
TK = 256
HQ = N_HEADS_A * TQ
INT_MIN = -2 ** 31
LOG2E = 1.4426950408889634

_F32 = jnp.float32
_BF16 = jnp.bfloat16


def _dot(a, b):
    return jnp.dot(a, b, preferred_element_type=_F32)


def _dot_nt(a, b):
    return lax.dot_general(a, b, (((1,), (1,)), ((), ())), preferred_element_type=_F32)


def _dot_tn(a, b):
    return lax.dot_general(a, b, (((0,), (0,)), ((), ())), preferred_element_type=_F32)


def _params(*sem):
    return pltpu.CompilerParams(dimension_semantics=sem, vmem_limit_bytes=VMEM_LIMIT_BYTES)


def _resident(shape):
    n = len(shape)
    return pl.BlockSpec(shape, lambda *_: (0,) * n, pipeline_mode=pl.Buffered(1))


def _in_proj_kernel(x_ref, g_ref, w_ref, b_ref, ws_ref, bs_ref, o16_ref, o32_ref, h_ref):
    @pl.when(pl.program_id(1) == 0)
    def _():
        x = x_ref[...]
        y = x * lax.rsqrt(jnp.mean(x * x, axis=-1, keepdims=True) + EPS)
        hb = (y * g_ref[...]).astype(_BF16)
        h_ref[...] = hb
        o32_ref[...] = _dot(hb, ws_ref[...]) + bs_ref[...]

    o16_ref[...] = (_dot(h_ref[...], w_ref[...]) + b_ref[...]).astype(_BF16)


def _in_proj(x2, g, w16, b16, w32, b32):
    n = x2.shape[0]
    tm, tn = 1024, 1024
    return pl.pallas_call(
        _in_proj_kernel,
        out_shape=(jax.ShapeDtypeStruct((n, N16), _BF16), jax.ShapeDtypeStruct((n, N32), _F32)),
        grid=(n // tm, N16 // tn),
        in_specs=[
            pl.BlockSpec((tm, D_MODEL), lambda i, j: (i, 0)),
            pl.BlockSpec((1, D_MODEL), lambda i, j: (0, 0)),
            pl.BlockSpec((D_MODEL, tn), lambda i, j: (0, j)),
            pl.BlockSpec((1, tn), lambda i, j: (0, j)),
            pl.BlockSpec((D_MODEL, N32), lambda i, j: (0, 0)),
            pl.BlockSpec((1, N32), lambda i, j: (0, 0)),
        ],
        out_specs=(
            pl.BlockSpec((tm, tn), lambda i, j: (i, j)),
            pl.BlockSpec((tm, N32), lambda i, j: (i, 0)),
        ),
        scratch_shapes=[pltpu.VMEM((tm, D_MODEL), _BF16)],
        compiler_params=_params("parallel", "arbitrary"),
        name="in_proj",
    )(x2, g, w16, b16, w32, b32)


def _log_sigmoid(x):
    return jnp.minimum(x, 0.0) - jnp.log(1.0 + jnp.exp(-jnp.abs(x)))


def _cumsum_rows(x):
    n = x.shape[0]
    row = lax.broadcasted_iota(jnp.int32, x.shape, 0)
    k = 1
    while k < n:
        x = x + jnp.where(row >= k, pltpu.roll(x, k, axis=0), 0.0)
        k *= 2
    return x


def _mlstm_kernel(qk_ref, v_ref, o_ref, g32_ref, cw_ref, hn_ref, out_ref,
                  ct_ref, n_ref, m_ref, tail_ref):
    @pl.when(pl.program_id(1) == 0)
    def _():
        ct_ref[...] = jnp.zeros_like(ct_ref)
        n_ref[...] = jnp.zeros_like(n_ref)
        m_ref[...] = jnp.zeros_like(m_ref)
        tail_ref[...] = jnp.zeros_like(tail_ref)

    L = ML
    qk_raw = qk_ref[...].astype(_F32)
    ext = jnp.concatenate([tail_ref[...], qk_raw], axis=0)
    tail_ref[...] = qk_raw[L - SUBLANES:, :]
    cw = cw_ref[...]
    conv = ext[SUBLANES:, :] * cw[CONV_W - 1:CONV_W, :]
    for j in range(1, CONV_W):
        conv = conv + pltpu.roll(ext, j, axis=0)[SUBLANES:, :] * cw[CONV_W - 1 - j:CONV_W - j, :]
    qk = conv * jax.nn.sigmoid(conv)

    g32 = g32_ref[...]
    bc = _cumsum_rows(_log_sigmoid(g32))
    g32_t = g32.T
    bc_t = bc.T
    row = lax.broadcasted_iota(jnp.int32, (L, L), 0)
    col = lax.broadcasted_iota(jnp.int32, (L, L), 1)
    causal = col <= row

    for h in range(N_HEADS_M):
        sl = slice(h * DH_M, (h + 1) * DH_M)
        q = qk[:, sl].astype(_BF16)
        k = (qk[:, W_M + h * DH_M:W_M + (h + 1) * DH_M] * (DH_M ** -0.5)).astype(_BF16)
        v = v_ref[:, sl]
        b_c = bc[:, C_MF + h:C_MF + h + 1]
        i_c = g32[:, C_MI + h:C_MI + h + 1]
        b_r = bc_t[C_MF + h:C_MF + h + 1, :]
        i_r = g32_t[C_MI + h:C_MI + h + 1, :]
        m_prev = m_ref[h]

        dlog = jnp.where(causal, b_c - b_r + i_r, NEG)
        inter = b_c + m_prev
        mj = jnp.maximum(inter, jnp.max(dlog, axis=-1, keepdims=True))
        dw = jnp.exp(dlog - mj)
        iw = jnp.exp(inter - mj)
        s = _dot_nt(q, k) * dw
        num = iw * _dot(q, ct_ref[h].astype(_BF16)) + _dot(s.astype(_BF16), v)
        qn = jnp.sum(q.astype(_F32) * n_ref[h], axis=-1, keepdims=True)
        den = iw * qn + jnp.sum(s, axis=-1, keepdims=True)
        hh = num / jnp.maximum(jnp.abs(den), jnp.exp(-mj))

        b_last = b_c[L - 1:L, :]
        gg = b_last - b_c + i_c
        m_new = jnp.maximum(b_last + m_prev, jnp.max(gg, axis=0, keepdims=True))
        decay = jnp.exp(b_last + m_prev - m_new)
        w = jnp.exp(gg - m_new)
        kf = k.astype(_F32)
        wv = (w * v.astype(_F32)).astype(_BF16)
        ct_ref[h] = decay * ct_ref[h] + _dot_tn(k, wv)
        n_ref[h] = decay * n_ref[h] + jnp.sum(w * kf, axis=0, keepdims=True)
        m_ref[h] = m_new

        y = hh * lax.rsqrt(jnp.mean(hh * hh, axis=-1, keepdims=True) + EPS) * hn_ref[:, sl]
        gate = jax.nn.sigmoid(o_ref[:, sl].astype(_F32))
        out_ref[:, sl] = (y * gate).astype(_BF16)


def _mlstm(p16, p32, conv_qk, norm_head, B, S):
    n = B * S
    nc = S // ML
    row = lambda b, c: b * nc + c
    return pl.pallas_call(
        _mlstm_kernel,
        out_shape=jax.ShapeDtypeStruct((n, W_M), _BF16),
        grid=(B, nc),
        in_specs=[
            pl.BlockSpec((ML, 2 * W_M), lambda b, c: (row(b, c), 0)),
            pl.BlockSpec((ML, W_M), lambda b, c: (row(b, c), OFF_MV // W_M)),
            pl.BlockSpec((ML, W_M), lambda b, c: (row(b, c), OFF_MO // W_M)),
            pl.BlockSpec((ML, N32), lambda b, c: (row(b, c), 0)),
            pl.BlockSpec((CONV_W, 2 * W_M), lambda b, c: (0, 0)),
            pl.BlockSpec((1, W_M), lambda b, c: (0, 0)),
        ],
        out_specs=pl.BlockSpec((ML, W_M), lambda b, c: (row(b, c), 0)),
        scratch_shapes=[
            pltpu.VMEM((N_HEADS_M, DH_M, DH_M), _F32),
            pltpu.VMEM((N_HEADS_M, 1, DH_M), _F32),
            pltpu.VMEM((N_HEADS_M, 1, 1), _F32),
            pltpu.VMEM((SUBLANES, 2 * W_M), _F32),
        ],
        compiler_params=_params("parallel", "arbitrary"),
        name="mlstm",
    )(p16, p16, p16, p32, conv_qk, norm_head)


def _rot_rows(x, half):
    return jnp.concatenate([x[half:], x[:half]], axis=0)


def _dsa_prep_kernel(aq_ref, iq_ref, ak_ref, av_ref, g32_ref, gk_ref,
                     cat_ref, sat_ref, cit_ref, sit_ref, ca_ref, sa_ref, ci_ref, si_ref,
                     qat_ref, qit_ref, wit_ref, ka_ref, ki_ref, vt_ref):
    cat, sat = cat_ref[...], sat_ref[...]
    aq_t = aq_ref[...].astype(_F32).T
    for h in range(N_HEADS_A):
        x = aq_t[h * DH_A:(h + 1) * DH_A]
        qat_ref[:, h * TQ:(h + 1) * TQ] = (x * cat + _rot_rows(x, DH_A // 2) * sat).astype(_BF16)

    cit, sit = cit_ref[...], sit_ref[...]
    iq_t = iq_ref[...].astype(_F32).T
    for h in range(N_HEADS_IDX):
        x = iq_t[h * DH_IDX:(h + 1) * DH_IDX]
        qit_ref[:, h * TQ:(h + 1) * TQ] = (x * cit + _rot_rows(x, DH_IDX // 2) * sit).astype(_BF16)

    g32 = g32_ref[...]
    w_t = g32.T[C_IW:C_IW + N_HEADS_IDX, :] * IDX_W_SCALE
    for h in range(N_HEADS_IDX):
        wit_ref[:, h * TQ:(h + 1) * TQ] = w_t[h:h + 1, :]

    ak = ak_ref[...].astype(_F32)
    ka_ref[...] = (ak * ca_ref[...] + pltpu.roll(ak, DH_A // 2, axis=1) * sa_ref[...]).astype(_BF16)

    ik = g32[:, C_IK:C_IK + DH_IDX]
    xc = ik - jnp.mean(ik, axis=-1, keepdims=True)
    ln = xc * lax.rsqrt(jnp.mean(xc * xc, axis=-1, keepdims=True) + EPS) * gk_ref[...]
    half = DH_IDX // 2
    ln_rot = jnp.concatenate([ln[:, half:], ln[:, :half]], axis=1)
    ki_ref[...] = (ln * ci_ref[...] + ln_rot * si_ref[...]).astype(_BF16)

    vt_ref[...] = av_ref[...].astype(_F32).T.astype(_BF16)


def _dsa_prep(p16, p32, norm_idx_k, tabs, B, S):
    nt = S // TQ
    row = lambda b, t: b * nt + t
    cat, sat, cit, sit, ca, sa, ci, si = tabs
    return pl.pallas_call(
        _dsa_prep_kernel,
        out_shape=(
            jax.ShapeDtypeStruct((B, DH_A, nt * HQ), _BF16),
            jax.ShapeDtypeStruct((B, DH_IDX, nt * HQ), _BF16),
            jax.ShapeDtypeStruct((B, 1, nt * HQ), _F32),
            jax.ShapeDtypeStruct((B, S, DH_A), _BF16),
            jax.ShapeDtypeStruct((B, S, DH_IDX), _BF16),
            jax.ShapeDtypeStruct((B, DH_A, S), _BF16),
        ),
        grid=(B, nt),
        in_specs=[
            pl.BlockSpec((TQ, W_A), lambda b, t: (row(b, t), OFF_AQ // W_A)),
            pl.BlockSpec((TQ, W_IDX), lambda b, t: (row(b, t), OFF_IQ // W_IDX)),
            pl.BlockSpec((TQ, DH_A), lambda b, t: (row(b, t), OFF_AK // DH_A)),
            pl.BlockSpec((TQ, DH_A), lambda b, t: (row(b, t), OFF_AV // DH_A)),
            pl.BlockSpec((TQ, N32), lambda b, t: (row(b, t), 0)),
            pl.BlockSpec((1, DH_IDX), lambda b, t: (0, 0)),
            pl.BlockSpec((DH_A, TQ), lambda b, t: (0, t)),
            pl.BlockSpec((DH_A, TQ), lambda b, t: (0, t)),
            pl.BlockSpec((DH_IDX, TQ), lambda b, t: (0, t)),
            pl.BlockSpec((DH_IDX, TQ), lambda b, t: (0, t)),
            pl.BlockSpec((TQ, DH_A), lambda b, t: (t, 0)),
            pl.BlockSpec((TQ, DH_A), lambda b, t: (t, 0)),
            pl.BlockSpec((TQ, DH_IDX), lambda b, t: (t, 0)),
            pl.BlockSpec((TQ, DH_IDX), lambda b, t: (t, 0)),
        ],
        out_specs=(
            pl.BlockSpec((None, DH_A, HQ), lambda b, t: (b, 0, t)),
            pl.BlockSpec((None, DH_IDX, HQ), lambda b, t: (b, 0, t)),
            pl.BlockSpec((None, 1, HQ), lambda b, t: (b, 0, t)),
            pl.BlockSpec((None, TQ, DH_A), lambda b, t: (b, t, 0)),
            pl.BlockSpec((None, TQ, DH_IDX), lambda b, t: (b, t, 0)),
            pl.BlockSpec((None, DH_A, TQ), lambda b, t: (b, 0, t)),
        ),
        compiler_params=_params("parallel", "parallel"),
        name="dsa_prep",
    )(p16, p16, p16, p16, p32, norm_idx_k, cat, sat, cit, sit, ca, sa, ci, si)


def _allsum_sublanes(x):
    for k in (4, 2, 1):
        x = x + pltpu.roll(x, k, axis=0)
    return x


def _key_to_f32(key):
    return pltpu.bitcast(key ^ ((key >> 31) & 0x7FFFFFFF), _F32)


def _dsa_kernel(qit_ref, wit_ref, qat_ref, ki_ref, ka_ref, vt_ref, out_ref,
                sc_ref, lg_ref, p_ref, acc_ref, jmax_ref, *, n_sel, idx_bits):
    t = pl.program_id(1)
    nkt = t + 1
    n_keys = nkt * TK
    grp = TK // SUBLANES
    q_chunk = (t * TQ + lax.broadcasted_iota(jnp.int32, (TK, TQ), 1)) // CHUNK
    row_in_tile = lax.broadcasted_iota(jnp.int32, (TK, TQ), 0)
    heads = [slice(h * TQ, (h + 1) * TQ) for h in range(N_HEADS_A)]

    def score_body(kt, carry):
        r0 = pl.multiple_of(kt * TK, TK)
        r_all = _dot(ki_ref[pl.ds(r0, TK), :], qit_ref[...])
        sc = jnp.zeros((TK, TQ), _F32)
        for sl in heads:
            sc = sc + wit_ref[:, sl] * jnp.maximum(r_all[:, sl], 0.0)
        k_chunk = (r0 + row_in_tile) // CHUNK
        sc = jnp.where(k_chunk <= q_chunk, sc, NEG)
        sc_ref[pl.ds(r0, TK), :] = jnp.where(sc == 0.0, 0.0, sc)
        return carry

    lax.fori_loop(0, nkt, score_body, 0)

    def count(pred):
        def body(kt, acc):
            r0 = pl.multiple_of(kt * TK, TK)
            blk = sc_ref[pl.ds(r0, TK), :]
            for g in range(grp):
                acc = acc + pred(blk[g * SUBLANES:(g + 1) * SUBLANES], r0 + g * SUBLANES)
            return acc
        return _allsum_sublanes(lax.fori_loop(0, nkt, body, jnp.zeros((SUBLANES, TQ), jnp.int32)))

    def count_ge(cand_key):
        cand = _key_to_f32(cand_key)
        return count(lambda blk, r: jnp.where(blk >= cand, 1, 0))

    c0 = count_ge(jnp.zeros((SUBLANES, TQ), jnp.int32))
    ok0 = c0 >= n_sel
    key0 = jnp.where(ok0, 0, INT_MIN)
    cge0 = jnp.where(ok0, c0, n_keys)

    def bit_step(i, carry):
        key, cge = carry
        cand = key + lax.shift_left(jnp.int32(1), 30 - i)
        c = count_ge(cand)
        ok = c >= n_sel
        return jnp.where(ok, cand, key), jnp.where(ok, c, cge)

    thr_key, cge = lax.fori_loop(0, 31, bit_step, (key0, cge0))
    thr = _key_to_f32(thr_key)

    sub = lax.broadcasted_iota(jnp.int32, (SUBLANES, TQ), 0)
    jmax_ref[...] = jnp.full_like(jmax_ref, 2 ** 30)

    @pl.when(jnp.max(cge) > n_sel)
    def _():
        need = n_sel - count(lambda blk, r: jnp.where(blk > thr, 1, 0))

        def step(i, j):
            cand = j + lax.shift_left(jnp.int32(1), idx_bits - 1 - i)
            c = count(lambda blk, r: jnp.where(blk == thr, jnp.where(r + sub < cand, 1, 0), 0))
            return jnp.where(c < need, cand, j)

        jmax_ref[...] = lax.fori_loop(0, idx_bits, step, jnp.zeros((SUBLANES, TQ), jnp.int32))

    thr_row = thr[0:1, :]
    jmax_row = jmax_ref[0:1, :]

    c2 = (DH_A ** -0.5) * LOG2E
    acc_ref[...] = jnp.zeros_like(acc_ref)

    def attn_body(kt, carry):
        m_old, l_old = carry
        r0 = pl.multiple_of(kt * TK, TK)
        blk = sc_ref[pl.ds(r0, TK), :]
        krow = r0 + row_in_tile
        picked = jnp.where(blk > thr_row, 1, jnp.where(blk == thr_row, jnp.where(krow <= jmax_row, 1, 0), 0))
        bias = jnp.where(jnp.where(krow // CHUNK <= q_chunk, picked, 0) > 0, 0.0, NEG)
        lg_ref[...] = _dot(ka_ref[pl.ds(r0, TK), :], qat_ref[...])
        m_parts, l_parts, a_parts = [], [], []
        for sl in heads:
            lgm = lg_ref[:, sl] * c2 + bias
            m_h = jnp.maximum(m_old[:, sl], jnp.max(lgm, axis=0, keepdims=True))
            a_h = jnp.exp2(m_old[:, sl] - m_h)
            p = jnp.exp2(lgm - m_h)
            p_ref[:, sl] = p.astype(_BF16)
            m_parts.append(m_h)
            a_parts.append(a_h)
            l_parts.append(a_h * l_old[:, sl] + jnp.sum(p, axis=0, keepdims=True))
        alpha = jnp.concatenate(a_parts, axis=1)
        acc_ref[...] = acc_ref[...] * alpha + _dot(vt_ref[:, pl.ds(r0, TK)], p_ref[...])
        return jnp.concatenate(m_parts, axis=1), jnp.concatenate(l_parts, axis=1)

    init = (jnp.full((1, HQ), NEG, _F32), jnp.zeros((1, HQ), _F32))
    _, l_fin = lax.fori_loop(0, nkt, attn_body, init)

    for h, sl in enumerate(heads):
        o = acc_ref[:, sl] / l_fin[:, sl]
        out_ref[:, h * DH_A:(h + 1) * DH_A] = o.T.astype(_BF16)


def _dsa(qat, qit, wit, ka, ki, vt, B, S):
    nt = S // TQ
    n_sel = min(TOPK_MAX, S // 4)
    idx_bits = max(1, int(np.ceil(np.log2(S))))
    kern = functools.partial(_dsa_kernel, n_sel=n_sel, idx_bits=idx_bits)
    return pl.pallas_call(
        kern,
        out_shape=jax.ShapeDtypeStruct((B * S, W_A), _BF16),
        grid=(B, nt),
        in_specs=[
            pl.BlockSpec((None, DH_IDX, HQ), lambda b, t: (b, 0, t)),
            pl.BlockSpec((None, 1, HQ), lambda b, t: (b, 0, t)),
            pl.BlockSpec((None, DH_A, HQ), lambda b, t: (b, 0, t)),
            pl.BlockSpec((None, S, DH_IDX), lambda b, t: (b, 0, 0)),
            pl.BlockSpec((None, S, DH_A), lambda b, t: (b, 0, 0)),
            pl.BlockSpec((None, DH_A, S), lambda b, t: (b, 0, 0)),
        ],
        out_specs=pl.BlockSpec((TQ, W_A), lambda b, t: (b * nt + t, 0)),
        scratch_shapes=[
            pltpu.VMEM((S, TQ), _F32),
            pltpu.VMEM((TK, HQ), _F32),
            pltpu.VMEM((TK, HQ), _BF16),
            pltpu.VMEM((DH_A, HQ), _F32),
            pltpu.VMEM((SUBLANES, TQ), jnp.int32),
        ],
        compiler_params=_params("parallel", "arbitrary"),
        name="dsa",
    )(qit, wit, qat, ki, ka, vt)


def _merge_kernel(hm_ref, ha_ref, gm_ref, ga_ref, x_ref, wbm_ref, wba_ref, wo_ref, g_ref, out_ref):
    ym = _dot(hm_ref[...], wbm_ref[...])
    ya = _dot(ha_ref[...], wba_ref[...])
    y = jax.nn.sigmoid(gm_ref[...].astype(_F32)) * ym + jax.nn.sigmoid(ga_ref[...].astype(_F32)) * ya
    z = _dot(y.astype(_BF16), wo_ref[...])
    zn = z * lax.rsqrt(jnp.mean(z * z, axis=-1, keepdims=True) + EPS) * g_ref[...]
    out_ref[...] = x_ref[...] + zn


def _merge(hm, ha, p16, x2, wbm, wba, wo, g):
    n = x2.shape[0]
    tm = 512
    rows = lambda c: pl.BlockSpec((tm, D_MODEL), lambda i: (i, c))
    return pl.pallas_call(
        _merge_kernel,
        out_shape=jax.ShapeDtypeStruct((n, D_MODEL), _F32),
        grid=(n // tm,),
        in_specs=[rows(0), rows(0), rows(OFF_GM // D_MODEL), rows(OFF_GA // D_MODEL), rows(0),
                  _resident((W_M, D_MODEL)), _resident((W_A, D_MODEL)), _resident((D_MODEL, D_MODEL)),
                  _resident((1, D_MODEL))],
        out_specs=rows(0),
        compiler_params=_params("parallel"),
        name="merge",
    )(hm, ha, p16, p16, x2, wbm, wba, wo, g)


def _ffn_kernel(x_ref, gpre_ref, wg_ref, wu_ref, wd_ref, gpost_ref, out_ref):
    x = x_ref[...]
    f = (x * lax.rsqrt(jnp.mean(x * x, axis=-1, keepdims=True) + EPS) * gpre_ref[...]).astype(_BF16)
    a = _dot(f, wg_ref[...])
    u = _dot(f, wu_ref[...])
    act = (a * jax.nn.sigmoid(a) * u).astype(_BF16)
    z = _dot(act, wd_ref[...])
    out_ref[...] = x + z * lax.rsqrt(jnp.mean(z * z, axis=-1, keepdims=True) + EPS) * gpost_ref[...]


def _ffn(x2, gpre, wg, wu, wd, gpost):
    n = x2.shape[0]
    tm = 512
    rows = pl.BlockSpec((tm, D_MODEL), lambda i: (i, 0))
    return pl.pallas_call(
        _ffn_kernel,
        out_shape=jax.ShapeDtypeStruct((n, D_MODEL), _F32),
        grid=(n // tm,),
        in_specs=[rows, _resident((1, D_MODEL)), _resident((D_MODEL, D_FF)), _resident((D_MODEL, D_FF)),
                  _resident((D_FF, D_MODEL)), _resident((1, D_MODEL))],
        out_specs=rows,
        compiler_params=_params("parallel"),
        name="ffn",
    )(x2, gpre, wg, wu, wd, gpost)


def _pack_in_proj(w, b):
    o = np.cumsum((0, W_M, W_M, W_M, W_M, N_HEADS_M, N_HEADS_M, W_A, DH_A, DH_A,
                   W_IDX, N_HEADS_IDX, DH_IDX, D_MODEL, D_MODEL))
    mq, mk, mv, mo, mi, mf, aq, ak, av, iq, iw, ik, gm, ga = [slice(o[i], o[i + 1]) for i in range(14)]
    order16 = (mq, mk, mv, mo, aq, gm, ga, iq, ak, av)
    order32 = (ik, iw, mi, mf)

    def cat(arr, order, width):
        parts = [arr[..., s] for s in order]
        used = sum(p.shape[-1] for p in parts)
        parts.append(jnp.zeros(arr.shape[:-1] + (width - used,), arr.dtype))
        return jnp.concatenate(parts, axis=-1)

    return (cat(w, order16, N16).astype(_BF16), cat(b, order16, N16)[None, :],
            cat(w, order32, N32).astype(_BF16), cat(b, order32, N32)[None, :])


def _rope_tables(S):
    def tab(dim):
        inv = ROPE_THETA ** (-jnp.arange(dim // 2, dtype=_F32) / (dim // 2))
        ang = jnp.arange(S, dtype=_F32)[:, None] * inv[None, :]
        c, s = jnp.cos(ang), jnp.sin(ang)
        return jnp.concatenate([c, c], axis=-1), jnp.concatenate([-s, s], axis=-1)

    ca, sa = tab(DH_A)
    ci, si = tab(DH_IDX)
    return (ca.T, sa.T, ci.T, si.T, ca, sa, ci, si)


def kernel(x, norm_mix_pre, norm_mix_post, norm_ffn_pre, norm_ffn_post, w_in, b_in, conv_qk,
           norm_mlstm_head, norm_idx_k, w_branch_mlstm, w_branch_attn, w_out,
           w_ffn_gate, w_ffn_up, w_ffn_down):
    B, S, _ = x.shape
    assert S % ML == 0 and S % TQ == 0 and TQ == TK and (B * S) % 1024 == 0
    depth = w_in.shape[0]
    tabs = _rope_tables(S)
    x2 = x.reshape(B * S, D_MODEL)
    for l in range(depth):
        w16, b16, w32, b32 = _pack_in_proj(w_in[l], b_in[l])
        p16, p32 = _in_proj(x2, norm_mix_pre[l][None, :], w16, b16, w32, b32)
        hm = _mlstm(p16, p32, conv_qk[l], norm_mlstm_head[l][None, :], B, S)
        qat, qit, wit, ka, ki, vt = _dsa_prep(p16, p32, norm_idx_k[l][None, :], tabs, B, S)
        ha = _dsa(qat, qit, wit, ka, ki, vt, B, S)
        x2 = _merge(hm, ha, p16, x2, w_branch_mlstm[l].astype(_BF16), w_branch_attn[l].astype(_BF16),
                    w_out[l].astype(_BF16), norm_mix_post[l][None, :])
        x2 = _ffn(x2, norm_ffn_pre[l][None, :], w_ffn_gate[l].astype(_BF16), w_ffn_up[l].astype(_BF16),
                  w_ffn_down[l].astype(_BF16), norm_ffn_post[l][None, :])
    return x2.reshape(B, S, D_MODEL)
```

```python
import functools

import jax
import jax.numpy as jnp
import numpy as np
from jax import lax
from jax.experimental import pallas as pl
from jax.experimental.pallas import tpu as pltpu

D_MODEL = 1024
CHUNK = 64
N_HEADS_M = 4
DH_M = 256
W_M = N_HEADS_M * DH_M
CONV_W = 4
N_HEADS_A = 8
DH_A = 128
W_A = N_HEADS_A * DH_A
N_HEADS_IDX = 8
DH_IDX = 64
W_IDX = N_HEADS_IDX * DH_IDX
IDX_W_SCALE = (N_HEADS_IDX ** -0.5) * (DH_IDX ** -0.5)
TOPK_MAX = 256
D_FF = 2816
ROPE_THETA = 10000.0
EPS = 1e-6
NEG = -1e30

LANES = 128
SUBLANES = 8
VMEM_LIMIT_BYTES = 56 * 1024 * 1024

OFF_MQ, OFF_MK, OFF_MV, OFF_MO = 0, 1024, 2048, 3072
OFF_AQ, OFF_GM, OFF_GA = 4096, 5120, 6144
OFF_IQ, OFF_AK, OFF_AV = 7168, 7680, 7808
N16 = 8192
C_IK, C_IW, C_MI, C_MF = 0, 64, 72, 76
N32 = 128

ML = 256
TQ = 256
TK = 256
TKB = 2 * TK
HQ = N_HEADS_A * TQ
VT_ROWS = DH_A + SUBLANES
INT_MIN = -2 ** 31
LOG2E = 1.4426950408889634
QK_SCALE2 = (DH_A ** -0.5) * LOG2E

_F32 = jnp.float32
_BF16 = jnp.bfloat16


def _dot(a, b):
    return jnp.dot(a, b, preferred_element_type=_F32)


def _dot_nt(a, b):
    return lax.dot_general(a, b, (((1,), (1,)), ((), ())), preferred_element_type=_F32)


def _dot_tn(a, b):
    return lax.dot_general(a, b, (((0,), (0,)), ((), ())), preferred_element_type=_F32)


def _params(*sem):
    return pltpu.CompilerParams(dimension_semantics=sem, vmem_limit_bytes=VMEM_LIMIT_BYTES)


def _resident(shape):
    n = len(shape)
    return pl.BlockSpec(shape, lambda *_: (0,) * n, pipeline_mode=pl.Buffered(1))


def _in_proj_kernel(x_ref, g_ref, w_ref, b_ref, ws_ref, bs_ref, o16_ref, o32_ref, h_ref):
    @pl.when(pl.program_id(1) == 0)
    def _():
        x = x_ref[...]
        y = x * lax.rsqrt(jnp.mean(x * x, axis=-1, keepdims=True) + EPS)
        hb = (y * g_ref[...]).astype(_BF16)
        h_ref[...] = hb
        o32_ref[...] = _dot(hb, ws_ref[...]) + bs_ref[...]

    o16_ref[...] = (_dot(h_ref[...], w_ref[...]) + b_ref[...]).astype(_BF16)


def _in_proj(x2, g, w16, b16, w32, b32):
    n = x2.shape[0]
    tm, tn = 1024, 1024
    return pl.pallas_call(
        _in_proj_kernel,
        out_shape=(jax.ShapeDtypeStruct((n, N16), _BF16), jax.ShapeDtypeStruct((n, N32), _F32)),
        grid=(n // tm, N16 // tn),
        in_specs=[
            pl.BlockSpec((tm, D_MODEL), lambda i, j: (i, 0)),
            pl.BlockSpec((1, D_MODEL), lambda i, j: (0, 0)),
            pl.BlockSpec((D_MODEL, tn), lambda i, j: (0, j)),
            pl.BlockSpec((1, tn), lambda i, j: (0, j)),
            pl.BlockSpec((D_MODEL, N32), lambda i, j: (0, 0)),
            pl.BlockSpec((1, N32), lambda i, j: (0, 0)),
        ],
        out_specs=(
            pl.BlockSpec((tm, tn), lambda i, j: (i, j)),
            pl.BlockSpec((tm, N32), lambda i, j: (i, 0)),
        ),
        scratch_shapes=[pltpu.VMEM((tm, D_MODEL), _BF16)],
        compiler_params=_params("parallel", "arbitrary"),
        name="in_proj",
    )(x2, g, w16, b16, w32, b32)


def _log_sigmoid(x):
    return jnp.minimum(x, 0.0) - jnp.log(1.0 + jnp.exp(-jnp.abs(x)))


def _cumsum_rows(x):
    n = x.shape[0]
    row = lax.broadcasted_iota(jnp.int32, x.shape, 0)
    k = 1
    while k < n:
        x = x + jnp.where(row >= k, pltpu.roll(x, k, axis=0), 0.0)
        k *= 2
    return x


def _mlstm_kernel(qk_ref, v_ref, o_ref, g32_ref, cw_ref, hn_ref, out_ref,
                  ct_ref, n_ref, m_ref, tail_ref):
    @pl.when(pl.program_id(1) == 0)
    def _():
        ct_ref[...] = jnp.zeros_like(ct_ref)
        n_ref[...] = jnp.zeros_like(n_ref)
        m_ref[...] = jnp.zeros_like(m_ref)
        tail_ref[...] = jnp.zeros_like(tail_ref)

    L = ML
    qk_raw = qk_ref[...].astype(_F32)
    ext = jnp.concatenate([tail_ref[...], qk_raw], axis=0)
    tail_ref[...] = qk_raw[L - SUBLANES:, :]
    cw = cw_ref[...]
    conv = ext[SUBLANES:, :] * cw[CONV_W - 1:CONV_W, :]
    for j in range(1, CONV_W):
        conv = conv + pltpu.roll(ext, j, axis=0)[SUBLANES:, :] * cw[CONV_W - 1 - j:CONV_W - j, :]
    qk = conv * jax.nn.sigmoid(conv)

    g32 = g32_ref[...]
    bc = _cumsum_rows(_log_sigmoid(g32))
    g32_t = g32.T
    bc_t = bc.T
    row = lax.broadcasted_iota(jnp.int32, (L, L), 0)
    col = lax.broadcasted_iota(jnp.int32, (L, L), 1)
    causal = col <= row

    for h in range(N_HEADS_M):
        sl = slice(h * DH_M, (h + 1) * DH_M)
        q = qk[:, sl].astype(_BF16)
        k = (qk[:, W_M + h * DH_M:W_M + (h + 1) * DH_M] * (DH_M ** -0.5)).astype(_BF16)
        v = v_ref[:, sl]
        b_c = bc[:, C_MF + h:C_MF + h + 1]
        i_c = g32[:, C_MI + h:C_MI + h + 1]
        b_r = bc_t[C_MF + h:C_MF + h + 1, :]
        i_r = g32_t[C_MI + h:C_MI + h + 1, :]
        m_prev = m_ref[h]

        dlog = jnp.where(causal, b_c - b_r + i_r, NEG)
        inter = b_c + m_prev
        mj = jnp.maximum(inter, jnp.max(dlog, axis=-1, keepdims=True))
        dw = jnp.exp(dlog - mj)
        iw = jnp.exp(inter - mj)
        s = _dot_nt(q, k) * dw
        num = iw * _dot(q, ct_ref[h].astype(_BF16)) + _dot(s.astype(_BF16), v)
        qn = jnp.sum(q.astype(_F32) * n_ref[h], axis=-1, keepdims=True)
        den = iw * qn + jnp.sum(s, axis=-1, keepdims=True)
        hh = num / jnp.maximum(jnp.abs(den), jnp.exp(-mj))

        b_last = b_c[L - 1:L, :]
        gg = b_last - b_c + i_c
        m_new = jnp.maximum(b_last + m_prev, jnp.max(gg, axis=0, keepdims=True))
        decay = jnp.exp(b_last + m_prev - m_new)
        w = jnp.exp(gg - m_new)
        kf = k.astype(_F32)
        wv = (w * v.astype(_F32)).astype(_BF16)
        ct_ref[h] = decay * ct_ref[h] + _dot_tn(k, wv)
        n_ref[h] = decay * n_ref[h] + jnp.sum(w * kf, axis=0, keepdims=True)
        m_ref[h] = m_new

        y = hh * lax.rsqrt(jnp.mean(hh * hh, axis=-1, keepdims=True) + EPS) * hn_ref[:, sl]
        gate = jax.nn.sigmoid(o_ref[:, sl].astype(_F32))
        out_ref[:, sl] = (y * gate).astype(_BF16)


def _mlstm(p16, p32, conv_qk, norm_head, B, S):
    n = B * S
    nc = S // ML
    row = lambda b, c: b * nc + c
    return pl.pallas_call(
        _mlstm_kernel,
        out_shape=jax.ShapeDtypeStruct((n, W_M), _BF16),
        grid=(B, nc),
        in_specs=[
            pl.BlockSpec((ML, 2 * W_M), lambda b, c: (row(b, c), 0)),
            pl.BlockSpec((ML, W_M), lambda b, c: (row(b, c), OFF_MV // W_M)),
            pl.BlockSpec((ML, W_M), lambda b, c: (row(b, c), OFF_MO // W_M)),
            pl.BlockSpec((ML, N32), lambda b, c: (row(b, c), 0)),
            pl.BlockSpec((CONV_W, 2 * W_M), lambda b, c: (0, 0)),
            pl.BlockSpec((1, W_M), lambda b, c: (0, 0)),
        ],
        out_specs=pl.BlockSpec((ML, W_M), lambda b, c: (row(b, c), 0)),
        scratch_shapes=[
            pltpu.VMEM((N_HEADS_M, DH_M, DH_M), _F32),
            pltpu.VMEM((N_HEADS_M, 1, DH_M), _F32),
            pltpu.VMEM((N_HEADS_M, 1, 1), _F32),
            pltpu.VMEM((SUBLANES, 2 * W_M), _F32),
        ],
        compiler_params=_params("parallel", "arbitrary"),
        name="mlstm",
    )(p16, p16, p16, p32, conv_qk, norm_head)


def _rot_rows(x, half):
    return jnp.concatenate([x[half:], x[:half]], axis=0)


def _dsa_prep_kernel(aq_ref, iq_ref, ak_ref, av_ref, g32_ref, gk_ref,
                     cat_ref, sat_ref, cit_ref, sit_ref, ca_ref, sa_ref, ci_ref, si_ref,
                     qat_ref, qit_ref, wit_ref, ka_ref, ki_ref, vt_ref):
    cat, sat = cat_ref[...], sat_ref[...]
    aq_t = aq_ref[...].astype(_F32).T
    for h in range(N_HEADS_A):
        x = aq_t[h * DH_A:(h + 1) * DH_A]
        qat_ref[:, h * TQ:(h + 1) * TQ] = ((x * cat + _rot_rows(x, DH_A // 2) * sat) * QK_SCALE2).astype(_BF16)

    cit, sit = cit_ref[...], sit_ref[...]
    iq_t = iq_ref[...].astype(_F32).T
    for h in range(N_HEADS_IDX):
        x = iq_t[h * DH_IDX:(h + 1) * DH_IDX]
        qit_ref[:, h * TQ:(h + 1) * TQ] = (x * cit + _rot_rows(x, DH_IDX // 2) * sit).astype(_BF16)

    g32 = g32_ref[...]
    w_t = g32.T[C_IW:C_IW + N_HEADS_IDX, :] * IDX_W_SCALE
    for h in range(N_HEADS_IDX):
        wit_ref[:, h * TQ:(h + 1) * TQ] = w_t[h:h + 1, :]

    ak = ak_ref[...].astype(_F32)
    ka_ref[...] = (ak * ca_ref[...] + pltpu.roll(ak, DH_A // 2, axis=1) * sa_ref[...]).astype(_BF16)

    ik = g32[:, C_IK:C_IK + DH_IDX]
    xc = ik - jnp.mean(ik, axis=-1, keepdims=True)
    ln = xc * lax.rsqrt(jnp.mean(xc * xc, axis=-1, keepdims=True) + EPS) * gk_ref[...]
    half = DH_IDX // 2
    ln_rot = jnp.concatenate([ln[:, half:], ln[:, :half]], axis=1)
    ki_ref[...] = (ln * ci_ref[...] + ln_rot * si_ref[...]).astype(_BF16)

    vt_ref[:DH_A, :] = av_ref[...].astype(_F32).T.astype(_BF16)
    vt_ref[DH_A:, :] = jnp.ones((VT_ROWS - DH_A, TQ), _BF16)


def _dsa_prep(p16, p32, norm_idx_k, tabs, B, S):
    nt = S // TQ
    row = lambda b, t: b * nt + t
    cat, sat, cit, sit, ca, sa, ci, si = tabs
    return pl.pallas_call(
        _dsa_prep_kernel,
        out_shape=(
            jax.ShapeDtypeStruct((B, DH_A, nt * HQ), _BF16),
            jax.ShapeDtypeStruct((B, DH_IDX, nt * HQ), _BF16),
            jax.ShapeDtypeStruct((B, 1, nt * HQ), _F32),
            jax.ShapeDtypeStruct((B, S, DH_A), _BF16),
            jax.ShapeDtypeStruct((B, S, DH_IDX), _BF16),
            jax.ShapeDtypeStruct((B, VT_ROWS, S), _BF16),
        ),
        grid=(B, nt),
        in_specs=[
            pl.BlockSpec((TQ, W_A), lambda b, t: (row(b, t), OFF_AQ // W_A)),
            pl.BlockSpec((TQ, W_IDX), lambda b, t: (row(b, t), OFF_IQ // W_IDX)),
            pl.BlockSpec((TQ, DH_A), lambda b, t: (row(b, t), OFF_AK // DH_A)),
            pl.BlockSpec((TQ, DH_A), lambda b, t: (row(b, t), OFF_AV // DH_A)),
            pl.BlockSpec((TQ, N32), lambda b, t: (row(b, t), 0)),
            pl.BlockSpec((1, DH_IDX), lambda b, t: (0, 0)),
            pl.BlockSpec((DH_A, TQ), lambda b, t: (0, t)),
            pl.BlockSpec((DH_A, TQ), lambda b, t: (0, t)),
            pl.BlockSpec((DH_IDX, TQ), lambda b, t: (0, t)),
            pl.BlockSpec((DH_IDX, TQ), lambda b, t: (0, t)),
            pl.BlockSpec((TQ, DH_A), lambda b, t: (t, 0)),
            pl.BlockSpec((TQ, DH_A), lambda b, t: (t, 0)),
            pl.BlockSpec((TQ, DH_IDX), lambda b, t: (t, 0)),
            pl.BlockSpec((TQ, DH_IDX), lambda b, t: (t, 0)),
        ],
        out_specs=(
            pl.BlockSpec((None, DH_A, HQ), lambda b, t: (b, 0, t)),
            pl.BlockSpec((None, DH_IDX, HQ), lambda b, t: (b, 0, t)),
            pl.BlockSpec((None, 1, HQ), lambda b, t: (b, 0, t)),
            pl.BlockSpec((None, TQ, DH_A), lambda b, t: (b, t, 0)),
            pl.BlockSpec((None, TQ, DH_IDX), lambda b, t: (b, t, 0)),
            pl.BlockSpec((None, VT_ROWS, TQ), lambda b, t: (b, 0, t)),
        ),
        compiler_params=_params("parallel", "parallel"),
        name="dsa_prep",
    )(p16, p16, p16, p16, p32, norm_idx_k, cat, sat, cit, sit, ca, sa, ci, si)


def _allsum_sublanes(x):
    for k in (4, 2, 1):
        x = x + pltpu.roll(x, k, axis=0)
    return x


def _key_to_f32(key):
    return pltpu.bitcast(key ^ ((key >> 31) & 0x7FFFFFFF), _F32)


def _dsa_kernel(qit_ref, wit_ref, qat_ref, ki_ref, ka_ref, vt_ref, out_ref,
                sc_ref, hi_ref, lo_ref, lg_ref, p_ref, acc_ref, m_ref, jmax_ref,
                *, n_sel, idx_bits):
    t = pl.program_id(1)
    nkt = t + 1
    n_keys = nkt * TK
    row_in_tile = lax.broadcasted_iota(jnp.int32, (TK, TQ), 0)
    lane_in_tile = lax.broadcasted_iota(jnp.int32, (TK, TQ), 1)
    adm_diag = (row_in_tile // CHUNK) <= (lane_in_tile // CHUNK)
    heads = [slice(h * TQ, (h + 1) * TQ) for h in range(N_HEADS_A)]
    tile = lambda kt: pl.ds(pl.multiple_of(kt * TK, TK), TK)
    i16 = jnp.int16
    P16 = 2 * SUBLANES

    def score_block(r0, n, diag):
        rows = pl.ds(r0, n)
        r_all = _dot(ki_ref[rows, :], qit_ref[...])
        sc = jnp.zeros((n, TQ), _F32)
        for sl in heads:
            sc = sc + wit_ref[:, sl] * jnp.maximum(r_all[:, sl], 0.0)
        if diag:
            sc = jnp.where(adm_diag, sc, NEG)
        sc = jnp.where(sc == 0.0, 0.0, sc)
        sc_ref[rows, :] = sc
        bits = pltpu.bitcast(sc, jnp.int32)
        key = bits ^ ((bits >> 31) & 0x7FFFFFFF)
        hi_ref[rows, :] = (key >> 16).astype(i16)
        lo_ref[rows, :] = ((key & 0xFFFF) - 32768).astype(i16)

    def score_body(i, carry):
        score_block(pl.multiple_of(i * TKB, TKB), TKB, False)
        return carry

    lax.fori_loop(0, t // 2, score_body, 0)
    pl.when(t % 2 == 1)(lambda: score_block(pl.multiple_of((t - 1) * TK, TK), TK, False))
    score_block(pl.multiple_of(t * TK, TK), TK, True)

    def count16(ref, pred):
        def body(kt, accs):
            blk = ref[tile(kt), :]
            accs = list(accs)
            for g in range(TK // P16):
                accs[g % 4] = accs[g % 4] + pred(blk[g * P16:(g + 1) * P16])
            return tuple(accs)
        z = jnp.zeros((P16, TQ), i16)
        a = [x.astype(jnp.int32) for x in lax.fori_loop(0, nkt, body, (z, z, z, z))]
        s = (a[0] + a[1]) + (a[2] + a[3])
        return _allsum_sublanes(s[:SUBLANES] + s[SUBLANES:])

    def rows16(x):
        return jnp.concatenate([x, x], axis=0).astype(i16)

    def count_ge16(ref, cand):
        c16 = rows16(cand)
        return count16(ref, lambda b: jnp.where(b >= c16, i16(1), i16(0)))

    def search16(ref, need, c_base):
        def step(i, carry):
            cur, cnt = carry
            cand = cur + lax.shift_left(jnp.int32(1), 15 - i)
            c = count_ge16(ref, cand)
            ok = c >= need
            return jnp.where(ok, cand, cur), jnp.where(ok, c, cnt)
        return lax.fori_loop(0, 16, step, (jnp.full((SUBLANES, TQ), -32768, jnp.int32), c_base))

    zeros8 = jnp.zeros((SUBLANES, TQ), jnp.int32)
    t_hi, c_hi = search16(hi_ref, zeros8 + n_sel, zeros8 + n_keys)
    c_gt = jnp.where(t_hi >= 32767, 0, count_ge16(hi_ref, jnp.minimum(t_hi + 1, 32767)))
    t_hi16 = rows16(t_hi)

    def narrow_body(kt, carry):
        rows = tile(kt)
        hi, lo = hi_ref[rows, :], lo_ref[rows, :]
        lo_ref[rows, :] = jnp.concatenate(
            [jnp.where(hi[g * P16:(g + 1) * P16] == t_hi16, lo[g * P16:(g + 1) * P16], i16(-32768))
             for g in range(TK // P16)], axis=0)
        return carry

    lax.fori_loop(0, nkt, narrow_body, 0)
    t_lo, c_lo = search16(lo_ref, n_sel - c_gt, c_hi - c_gt)
    cge = c_gt + c_lo
    thr = _key_to_f32((t_hi << 16) | ((t_lo + 32768) & 0xFFFF))
    thr_row = thr[0:1, :]

    def count32(pred):
        def body(kt, accs):
            r0 = pl.multiple_of(kt * TK, TK)
            blk = sc_ref[pl.ds(r0, TK), :]
            accs = list(accs)
            for g in range(TK // SUBLANES):
                accs[g % 4] = accs[g % 4] + pred(blk[g * SUBLANES:(g + 1) * SUBLANES], r0 + g * SUBLANES)
            return tuple(accs)
        a = lax.fori_loop(0, nkt, body, (zeros8, zeros8, zeros8, zeros8))
        return _allsum_sublanes((a[0] + a[1]) + (a[2] + a[3]))

    sub = lax.broadcasted_iota(jnp.int32, (SUBLANES, TQ), 0)
    jmax_ref[...] = jnp.full_like(jmax_ref, 2 ** 30)
    has_ties = jnp.max(cge) > n_sel

    @pl.when(has_ties)
    def _():
        need = n_sel - count32(lambda blk, r: jnp.where(blk > thr, 1, 0))

        def step(i, j):
            cand = j + lax.shift_left(jnp.int32(1), idx_bits - 1 - i)
            c = count32(lambda blk, r: jnp.where(blk == thr, jnp.where(r + sub < cand, 1, 0), 0))
            return jnp.where(c < need, cand, j)

        jmax_ref[...] = lax.fori_loop(0, idx_bits, step, zeros8)

    def bias_tile(kt, general, diag):
        rows = tile(kt)
        blk = sc_ref[rows, :]
        if general:
            krow = kt * TK + row_in_tile
            pick = jnp.where(blk > thr_row, 1,
                             jnp.where(blk == thr_row, jnp.where(krow <= jmax_ref[0:1, :], 1, 0), 0))
            if diag:
                pick = jnp.where(adm_diag, pick, 0)
            sc_ref[rows, :] = jnp.where(pick > 0, 0.0, NEG)
        else:
            sc_ref[rows, :] = jnp.where(blk >= thr_row, 0.0, NEG)

    def bias_loop(general):
        def body(kt, carry):
            bias_tile(kt, general, False)
            return carry
        lax.fori_loop(0, t, body, 0)

    pl.when(has_ties)(lambda: bias_loop(True))
    pl.when(jnp.logical_not(has_ties))(lambda: bias_loop(False))
    bias_tile(t, True, True)

    acc_ref[...] = jnp.zeros_like(acc_ref)
    m_ref[...] = jnp.full_like(m_ref, NEG)

    def attn_body(kt, carry):
        rows = tile(kt)
        bias = sc_ref[rows, :]
        ka = ka_ref[rows, :]
        vt = vt_ref[:, rows]
        m_all = m_ref[...]
        m_rows = []
        for h, sl in enumerate(heads):
            x = _dot(ka, qat_ref[:, sl]) + bias
            lg_ref[:, sl] = x
            m_rows.append(jnp.maximum(m_all[h:h + 1, :], jnp.max(x, axis=0, keepdims=True)))
        for h, sl in enumerate(heads):
            p_ref[:, sl] = jnp.exp2(lg_ref[:, sl] - m_rows[h]).astype(_BF16)
            alpha = jnp.exp2(m_all[h:h + 1, :] - m_rows[h])
            acc_ref[:, sl] = acc_ref[:, sl] * alpha + _dot(vt, p_ref[:, sl])
        m_ref[...] = jnp.concatenate(m_rows, axis=0)
        return carry

    lax.fori_loop(0, nkt, attn_body, 0)

    for h, sl in enumerate(heads):
        o = acc_ref[:DH_A, sl] / acc_ref[DH_A:DH_A + 1, sl]
        out_ref[:, h * DH_A:(h + 1) * DH_A] = o.T.astype(_BF16)


def _dsa(qat, qit, wit, ka, ki, vt, B, S):
    nt = S // TQ
    n_sel = min(TOPK_MAX, S // 4)
    idx_bits = max(1, int(np.ceil(np.log2(S))))
    kern = functools.partial(_dsa_kernel, n_sel=n_sel, idx_bits=idx_bits)
    return pl.pallas_call(
        kern,
        out_shape=jax.ShapeDtypeStruct((B * S, W_A), _BF16),
        grid=(B, nt),
        in_specs=[
            pl.BlockSpec((None, DH_IDX, HQ), lambda b, t: (b, 0, t)),
            pl.BlockSpec((None, 1, HQ), lambda b, t: (b, 0, t)),
            pl.BlockSpec((None, DH_A, HQ), lambda b, t: (b, 0, t)),
            pl.BlockSpec((None, S, DH_IDX), lambda b, t: (b, 0, 0)),
            pl.BlockSpec((None, S, DH_A), lambda b, t: (b, 0, 0)),
            pl.BlockSpec((None, VT_ROWS, S), lambda b, t: (b, 0, 0)),
        ],
        out_specs=pl.BlockSpec((TQ, W_A), lambda b, t: (b * nt + t, 0)),
        scratch_shapes=[
            pltpu.VMEM((S, TQ), _F32),
            pltpu.VMEM((S, TQ), jnp.int16),
            pltpu.VMEM((S, TQ), jnp.int16),
            pltpu.VMEM((TK, HQ), _F32),
            pltpu.VMEM((TK, HQ), _BF16),
            pltpu.VMEM((VT_ROWS, HQ), _F32),
            pltpu.VMEM((N_HEADS_A, TQ), _F32),
            pltpu.VMEM((SUBLANES, TQ), jnp.int32),
        ],
        compiler_params=_params("parallel", "arbitrary"),
        name="dsa",
    )(qit, wit, qat, ki, ka, vt)


def _merge_kernel(hm_ref, ha_ref, gm_ref, ga_ref, x_ref, wbm_ref, wba_ref, wo_ref, g_ref, out_ref):
    ym = _dot(hm_ref[...], wbm_ref[...])
    ya = _dot(ha_ref[...], wba_ref[...])
    y = jax.nn.sigmoid(gm_ref[...].astype(_F32)) * ym + jax.nn.sigmoid(ga_ref[...].astype(_F32)) * ya
    z = _dot(y.astype(_BF16), wo_ref[...])
    zn = z * lax.rsqrt(jnp.mean(z * z, axis=-1, keepdims=True) + EPS) * g_ref[...]
    out_ref[...] = x_ref[...] + zn


def _merge(hm, ha, p16, x2, wbm, wba, wo, g):
    n = x2.shape[0]
    tm = 512
    rows = lambda c: pl.BlockSpec((tm, D_MODEL), lambda i: (i, c))
    return pl.pallas_call(
        _merge_kernel,
        out_shape=jax.ShapeDtypeStruct((n, D_MODEL), _F32),
        grid=(n // tm,),
        in_specs=[rows(0), rows(0), rows(OFF_GM // D_MODEL), rows(OFF_GA // D_MODEL), rows(0),
                  _resident((W_M, D_MODEL)), _resident((W_A, D_MODEL)), _resident((D_MODEL, D_MODEL)),
                  _resident((1, D_MODEL))],
        out_specs=rows(0),
        compiler_params=_params("parallel"),
        name="merge",
    )(hm, ha, p16, p16, x2, wbm, wba, wo, g)


def _ffn_kernel(x_ref, gpre_ref, wg_ref, wu_ref, wd_ref, gpost_ref, out_ref):
    x = x_ref[...]
    f = (x * lax.rsqrt(jnp.mean(x * x, axis=-1, keepdims=True) + EPS) * gpre_ref[...]).astype(_BF16)
    a = _dot(f, wg_ref[...])
    u = _dot(f, wu_ref[...])
    act = (a * jax.nn.sigmoid(a) * u).astype(_BF16)
    z = _dot(act, wd_ref[...])
    out_ref[...] = x + z * lax.rsqrt(jnp.mean(z * z, axis=-1, keepdims=True) + EPS) * gpost_ref[...]


def _ffn(x2, gpre, wg, wu, wd, gpost):
    n = x2.shape[0]
    tm = 512
    rows = pl.BlockSpec((tm, D_MODEL), lambda i: (i, 0))
    return pl.pallas_call(
        _ffn_kernel,
        out_shape=jax.ShapeDtypeStruct((n, D_MODEL), _F32),
        grid=(n // tm,),
        in_specs=[rows, _resident((1, D_MODEL)), _resident((D_MODEL, D_FF)), _resident((D_MODEL, D_FF)),
                  _resident((D_FF, D_MODEL)), _resident((1, D_MODEL))],
        out_specs=rows,
        compiler_params=_params("parallel"),
        name="ffn",
    )(x2, gpre, wg, wu, wd, gpost)


def _pack_in_proj(w, b):
    o = np.cumsum((0, W_M, W_M, W_M, W_M, N_HEADS_M, N_HEADS_M, W_A, DH_A, DH_A,
                   W_IDX, N_HEADS_IDX, DH_IDX, D_MODEL, D_MODEL))
    mq, mk, mv, mo, mi, mf, aq, ak, av, iq, iw, ik, gm, ga = [slice(o[i], o[i + 1]) for i in range(14)]
    order16 = (mq, mk, mv, mo, aq, gm, ga, iq, ak, av)
    order32 = (ik, iw, mi, mf)

    def cat(arr, order, width):
        parts = [arr[..., s] for s in order]
        used = sum(p.shape[-1] for p in parts)
        parts.append(jnp.zeros(arr.shape[:-1] + (width - used,), arr.dtype))
        return jnp.concatenate(parts, axis=-1)

    return (cat(w, order16, N16).astype(_BF16), cat(b, order16, N16)[None, :],
            cat(w, order32, N32).astype(_BF16), cat(b, order32, N32)[None, :])


def _rope_tables(S):
    def tab(dim):
        inv = ROPE_THETA ** (-jnp.arange(dim // 2, dtype=_F32) / (dim // 2))
        ang = jnp.arange(S, dtype=_F32)[:, None] * inv[None, :]
        c, s = jnp.cos(ang), jnp.sin(ang)
        return jnp.concatenate([c, c], axis=-1), jnp.concatenate([-s, s], axis=-1)

    ca, sa = tab(DH_A)
    ci, si = tab(DH_IDX)
    return (ca.T, sa.T, ci.T, si.T, ca, sa, ci, si)


def kernel(x, norm_mix_pre, norm_mix_post, norm_ffn_pre, norm_ffn_post, w_in, b_in, conv_qk,
           norm_mlstm_head, norm_idx_k, w_branch_mlstm, w_branch_attn, w_out,
           w_ffn_gate, w_ffn_up, w_ffn_down):
    B, S, _ = x.shape
    assert S % ML == 0 and S % TKB == 0 and TQ == TK and (B * S) % 1024 == 0
    depth = w_in.shape[0]
    tabs = _rope_tables(S)
    x2 = x.reshape(B * S, D_MODEL)
    for l in range(depth):
        w16, b16, w32, b32 = _pack_in_proj(w_in[l], b_in[l])
        p16, p32 = _in_proj(x2, norm_mix_pre[l][None, :], w16, b16, w32, b32)
        hm = _mlstm(p16, p32, conv_qk[l], norm_mlstm_head[l][None, :], B, S)
        qat, qit, wit, ka, ki, vt = _dsa_prep(p16, p32, norm_idx_k[l][None, :], tabs, B, S)
        ha = _dsa(qat, qit, wit, ka, ki, vt, B, S)
        x2 = _merge(hm, ha, p16, x2, w_branch_mlstm[l].astype(_BF16), w_branch_attn[l].astype(_BF16),
                    w_out[l].astype(_BF16), norm_mix_post[l][None, :])
        x2 = _ffn(x2, norm_ffn_pre[l][None, :], w_ffn_gate[l].astype(_BF16), w_ffn_up[l].astype(_BF16),
                  w_ffn_down[l].astype(_BF16), norm_ffn_post[l][None, :])
    return x2.reshape(B, S, D_MODEL)
```

```python
import functools

import jax
import jax.numpy as jnp
import numpy as np
from jax import lax
from jax.experimental import pallas as pl
from jax.experimental.pallas import tpu as pltpu

D_MODEL = 1024
CHUNK = 64
N_HEADS_M = 4
DH_M = 256
W_M = N_HEADS_M * DH_M
CONV_W = 4
N_HEADS_A = 8
DH_A = 128
W_A = N_HEADS_A * DH_A
N_HEADS_IDX = 8
DH_IDX = 64
W_IDX = N_HEADS_IDX * DH_IDX
IDX_W_SCALE = (N_HEADS_IDX ** -0.5) * (DH_IDX ** -0.5)
TOPK_MAX = 256
D_FF = 2816
ROPE_THETA = 10000.0
EPS = 1e-6
NEG = -1e30

LANES = 128
SUBLANES = 8
VMEM_LIMIT_BYTES = 56 * 1024 * 1024

OFF_MQ, OFF_MK, OFF_MV, OFF_MO = 0, 1024, 2048, 3072
OFF_AQ, OFF_GM, OFF_GA = 4096, 5120, 6144
OFF_IQ, OFF_AK, OFF_AV = 7168, 7680, 7808
N16 = 8192
C_IK, C_IW, C_MI, C_MF = 0, 64, 72, 76
N32 = 128

ML = 256
TQ = 256
TK = 256
TKB = 2 * TK
HQ = N_HEADS_A * TQ
VT_ROWS = DH_A + SUBLANES
INT_MIN = -2 ** 31
LOG2E = 1.4426950408889634
QK_SCALE2 = (DH_A ** -0.5) * LOG2E

_F32 = jnp.float32
_BF16 = jnp.bfloat16


def _dot(a, b):
    return jnp.dot(a, b, preferred_element_type=_F32)


def _dot_nt(a, b):
    return lax.dot_general(a, b, (((1,), (1,)), ((), ())), preferred_element_type=_F32)


def _dot_tn(a, b):
    return lax.dot_general(a, b, (((0,), (0,)), ((), ())), preferred_element_type=_F32)


def _params(*sem):
    return pltpu.CompilerParams(dimension_semantics=sem, vmem_limit_bytes=VMEM_LIMIT_BYTES)


def _resident(shape):
    n = len(shape)
    return pl.BlockSpec(shape, lambda *_: (0,) * n, pipeline_mode=pl.Buffered(1))


def _in_proj_kernel(x_ref, g_ref, w_ref, b_ref, ws_ref, bs_ref, o16_ref, o32_ref, h_ref):
    @pl.when(pl.program_id(1) == 0)
    def _():
        x = x_ref[...]
        y = x * lax.rsqrt(jnp.mean(x * x, axis=-1, keepdims=True) + EPS)
        hb = (y * g_ref[...]).astype(_BF16)
        h_ref[...] = hb
        o32_ref[...] = _dot(hb, ws_ref[...]) + bs_ref[...]

    o16_ref[...] = (_dot(h_ref[...], w_ref[...]) + b_ref[...]).astype(_BF16)


def _in_proj(x2, g, w16, b16, w32, b32):
    n = x2.shape[0]
    tm, tn = 1024, 2048
    return pl.pallas_call(
        _in_proj_kernel,
        out_shape=(jax.ShapeDtypeStruct((n, N16), _BF16), jax.ShapeDtypeStruct((n, N32), _F32)),
        grid=(n // tm, N16 // tn),
        in_specs=[
            pl.BlockSpec((tm, D_MODEL), lambda i, j: (i, 0)),
            pl.BlockSpec((1, D_MODEL), lambda i, j: (0, 0)),
            pl.BlockSpec((D_MODEL, tn), lambda i, j: (0, j)),
            pl.BlockSpec((1, tn), lambda i, j: (0, j)),
            pl.BlockSpec((D_MODEL, N32), lambda i, j: (0, 0)),
            pl.BlockSpec((1, N32), lambda i, j: (0, 0)),
        ],
        out_specs=(
            pl.BlockSpec((tm, tn), lambda i, j: (i, j)),
            pl.BlockSpec((tm, N32), lambda i, j: (i, 0)),
        ),
        scratch_shapes=[pltpu.VMEM((tm, D_MODEL), _BF16)],
        compiler_params=_params("parallel", "arbitrary"),
        name="in_proj",
    )(x2, g, w16, b16, w32, b32)


def _log_sigmoid(x):
    return jnp.minimum(x, 0.0) - jnp.log(1.0 + jnp.exp(-jnp.abs(x)))


def _cumsum_rows(x):
    n = x.shape[0]
    row = lax.broadcasted_iota(jnp.int32, x.shape, 0)
    k = 1
    while k < n:
        x = x + jnp.where(row >= k, pltpu.roll(x, k, axis=0), 0.0)
        k *= 2
    return x


def _mlstm_kernel(qk_ref, v_ref, o_ref, g32_ref, cw_ref, hn_ref, out_ref,
                  ct_ref, n_ref, m_ref, tail_ref):
    @pl.when(pl.program_id(1) == 0)
    def _():
        ct_ref[...] = jnp.zeros_like(ct_ref)
        n_ref[...] = jnp.zeros_like(n_ref)
        m_ref[...] = jnp.zeros_like(m_ref)
        tail_ref[...] = jnp.zeros_like(tail_ref)

    L = ML
    qk_raw = qk_ref[...].astype(_F32)
    ext = jnp.concatenate([tail_ref[...], qk_raw], axis=0)
    tail_ref[...] = qk_raw[L - SUBLANES:, :]
    cw = cw_ref[...]
    conv = ext[SUBLANES:, :] * cw[CONV_W - 1:CONV_W, :]
    for j in range(1, CONV_W):
        conv = conv + pltpu.roll(ext, j, axis=0)[SUBLANES:, :] * cw[CONV_W - 1 - j:CONV_W - j, :]
    qk = conv * jax.nn.sigmoid(conv)

    g32 = g32_ref[...]
    bc = _cumsum_rows(_log_sigmoid(g32))
    g32_t = g32.T
    bc_t = bc.T
    row = lax.broadcasted_iota(jnp.int32, (L, L), 0)
    col = lax.broadcasted_iota(jnp.int32, (L, L), 1)
    causal = col <= row

    for h in range(N_HEADS_M):
        sl = slice(h * DH_M, (h + 1) * DH_M)
        q = qk[:, sl].astype(_BF16)
        k = (qk[:, W_M + h * DH_M:W_M + (h + 1) * DH_M] * (DH_M ** -0.5)).astype(_BF16)
        v = v_ref[:, sl]
        b_c = bc[:, C_MF + h:C_MF + h + 1]
        i_c = g32[:, C_MI + h:C_MI + h + 1]
        b_r = bc_t[C_MF + h:C_MF + h + 1, :]
        i_r = g32_t[C_MI + h:C_MI + h + 1, :]
        m_prev = m_ref[h]

        dlog = jnp.where(causal, b_c - b_r + i_r, NEG)
        inter = b_c + m_prev
        mj = jnp.maximum(inter, jnp.max(dlog, axis=-1, keepdims=True))
        dw = jnp.exp(dlog - mj)
        iw = jnp.exp(inter - mj)
        s = _dot_nt(q, k) * dw
        num = iw * _dot(q, ct_ref[h].astype(_BF16)) + _dot(s.astype(_BF16), v)
        qn = jnp.sum(q.astype(_F32) * n_ref[h], axis=-1, keepdims=True)
        den = iw * qn + jnp.sum(s, axis=-1, keepdims=True)
        hh = num / jnp.maximum(jnp.abs(den), jnp.exp(-mj))

        b_last = b_c[L - 1:L, :]
        gg = b_last - b_c + i_c
        m_new = jnp.maximum(b_last + m_prev, jnp.max(gg, axis=0, keepdims=True))
        decay = jnp.exp(b_last + m_prev - m_new)
        w = jnp.exp(gg - m_new)
        kf = k.astype(_F32)
        wv = (w * v.astype(_F32)).astype(_BF16)
        ct_ref[h] = decay * ct_ref[h] + _dot_tn(k, wv)
        n_ref[h] = decay * n_ref[h] + jnp.sum(w * kf, axis=0, keepdims=True)
        m_ref[h] = m_new

        y = hh * lax.rsqrt(jnp.mean(hh * hh, axis=-1, keepdims=True) + EPS) * hn_ref[:, sl]
        gate = jax.nn.sigmoid(o_ref[:, sl].astype(_F32))
        out_ref[:, sl] = (y * gate).astype(_BF16)


def _mlstm(p16, p32, conv_qk, norm_head, B, S):
    n = B * S
    nc = S // ML
    row = lambda b, c: b * nc + c
    return pl.pallas_call(
        _mlstm_kernel,
        out_shape=jax.ShapeDtypeStruct((n, W_M), _BF16),
        grid=(B, nc),
        in_specs=[
            pl.BlockSpec((ML, 2 * W_M), lambda b, c: (row(b, c), 0)),
            pl.BlockSpec((ML, W_M), lambda b, c: (row(b, c), OFF_MV // W_M)),
            pl.BlockSpec((ML, W_M), lambda b, c: (row(b, c), OFF_MO // W_M)),
            pl.BlockSpec((ML, N32), lambda b, c: (row(b, c), 0)),
            pl.BlockSpec((CONV_W, 2 * W_M), lambda b, c: (0, 0)),
            pl.BlockSpec((1, W_M), lambda b, c: (0, 0)),
        ],
        out_specs=pl.BlockSpec((ML, W_M), lambda b, c: (row(b, c), 0)),
        scratch_shapes=[
            pltpu.VMEM((N_HEADS_M, DH_M, DH_M), _F32),
            pltpu.VMEM((N_HEADS_M, 1, DH_M), _F32),
            pltpu.VMEM((N_HEADS_M, 1, 1), _F32),
            pltpu.VMEM((SUBLANES, 2 * W_M), _F32),
        ],
        compiler_params=_params("parallel", "arbitrary"),
        name="mlstm",
    )(p16, p16, p16, p32, conv_qk, norm_head)


def _rot_rows(x, half):
    return jnp.concatenate([x[half:], x[:half]], axis=0)


def _dsa_prep_kernel(aq_ref, iq_ref, ak_ref, av_ref, g32_ref, gk_ref,
                     cat_ref, sat_ref, cit_ref, sit_ref, ca_ref, sa_ref, ci_ref, si_ref,
                     qat_ref, qit_ref, wit_ref, ka_ref, ki_ref, vt_ref):
    cat, sat = cat_ref[...], sat_ref[...]
    aq_t = aq_ref[...].astype(_F32).T
    for h in range(N_HEADS_A):
        x = aq_t[h * DH_A:(h + 1) * DH_A]
        qat_ref[:, h * TQ:(h + 1) * TQ] = ((x * cat + _rot_rows(x, DH_A // 2) * sat) * QK_SCALE2).astype(_BF16)

    cit, sit = cit_ref[...], sit_ref[...]
    iq_t = iq_ref[...].astype(_F32).T
    for h in range(N_HEADS_IDX):
        x = iq_t[h * DH_IDX:(h + 1) * DH_IDX]
        qit_ref[:, h * TQ:(h + 1) * TQ] = (x * cit + _rot_rows(x, DH_IDX // 2) * sit).astype(_BF16)

    g32 = g32_ref[...]
    w_t = g32.T[C_IW:C_IW + N_HEADS_IDX, :] * IDX_W_SCALE
    for h in range(N_HEADS_IDX):
        wit_ref[:, h * TQ:(h + 1) * TQ] = w_t[h:h + 1, :]

    ak = ak_ref[...].astype(_F32)
    ka_ref[...] = (ak * ca_ref[...] + pltpu.roll(ak, DH_A // 2, axis=1) * sa_ref[...]).astype(_BF16)

    ik = g32[:, C_IK:C_IK + DH_IDX]
    xc = ik - jnp.mean(ik, axis=-1, keepdims=True)
    ln = xc * lax.rsqrt(jnp.mean(xc * xc, axis=-1, keepdims=True) + EPS) * gk_ref[...]
    half = DH_IDX // 2
    ln_rot = jnp.concatenate([ln[:, half:], ln[:, :half]], axis=1)
    ki_ref[...] = (ln * ci_ref[...] + ln_rot * si_ref[...]).astype(_BF16)

    vt_ref[:DH_A, :] = av_ref[...].astype(_F32).T.astype(_BF16)
    vt_ref[DH_A:, :] = jnp.ones((VT_ROWS - DH_A, TQ), _BF16)


def _dsa_prep(p16, p32, norm_idx_k, tabs, B, S):
    nt = S // TQ
    row = lambda b, t: b * nt + t
    cat, sat, cit, sit, ca, sa, ci, si = tabs
    return pl.pallas_call(
        _dsa_prep_kernel,
        out_shape=(
            jax.ShapeDtypeStruct((B, DH_A, nt * HQ), _BF16),
            jax.ShapeDtypeStruct((B, DH_IDX, nt * HQ), _BF16),
            jax.ShapeDtypeStruct((B, 1, nt * HQ), _F32),
            jax.ShapeDtypeStruct((B, S, DH_A), _BF16),
            jax.ShapeDtypeStruct((B, S, DH_IDX), _BF16),
            jax.ShapeDtypeStruct((B, VT_ROWS, S), _BF16),
        ),
        grid=(B, nt),
        in_specs=[
            pl.BlockSpec((TQ, W_A), lambda b, t: (row(b, t), OFF_AQ // W_A)),
            pl.BlockSpec((TQ, W_IDX), lambda b, t: (row(b, t), OFF_IQ // W_IDX)),
            pl.BlockSpec((TQ, DH_A), lambda b, t: (row(b, t), OFF_AK // DH_A)),
            pl.BlockSpec((TQ, DH_A), lambda b, t: (row(b, t), OFF_AV // DH_A)),
            pl.BlockSpec((TQ, N32), lambda b, t: (row(b, t), 0)),
            pl.BlockSpec((1, DH_IDX), lambda b, t: (0, 0)),
            pl.BlockSpec((DH_A, TQ), lambda b, t: (0, t)),
            pl.BlockSpec((DH_A, TQ), lambda b, t: (0, t)),
            pl.BlockSpec((DH_IDX, TQ), lambda b, t: (0, t)),
            pl.BlockSpec((DH_IDX, TQ), lambda b, t: (0, t)),
            pl.BlockSpec((TQ, DH_A), lambda b, t: (t, 0)),
            pl.BlockSpec((TQ, DH_A), lambda b, t: (t, 0)),
            pl.BlockSpec((TQ, DH_IDX), lambda b, t: (t, 0)),
            pl.BlockSpec((TQ, DH_IDX), lambda b, t: (t, 0)),
        ],
        out_specs=(
            pl.BlockSpec((None, DH_A, HQ), lambda b, t: (b, 0, t)),
            pl.BlockSpec((None, DH_IDX, HQ), lambda b, t: (b, 0, t)),
            pl.BlockSpec((None, 1, HQ), lambda b, t: (b, 0, t)),
            pl.BlockSpec((None, TQ, DH_A), lambda b, t: (b, t, 0)),
            pl.BlockSpec((None, TQ, DH_IDX), lambda b, t: (b, t, 0)),
            pl.BlockSpec((None, VT_ROWS, TQ), lambda b, t: (b, 0, t)),
        ),
        compiler_params=_params("parallel", "parallel"),
        name="dsa_prep",
    )(p16, p16, p16, p16, p32, norm_idx_k, cat, sat, cit, sit, ca, sa, ci, si)


def _allsum_sublanes(x):
    for k in (4, 2, 1):
        x = x + pltpu.roll(x, k, axis=0)
    return x


def _key_to_f32(key):
    return pltpu.bitcast(key ^ ((key >> 31) & 0x7FFFFFFF), _F32)


def _dsa_kernel(qit_ref, wit_ref, qat_ref, ki_ref, ka_ref, vt_ref, out_ref,
                sc_ref, hi_ref, lo_ref, lg_ref, p_ref, acc_ref, m_ref, jmax_ref,
                *, n_sel, idx_bits):
    t = pl.program_id(1)
    nkt = t + 1
    n_keys = nkt * TK
    row_in_tile = lax.broadcasted_iota(jnp.int32, (TK, TQ), 0)
    lane_in_tile = lax.broadcasted_iota(jnp.int32, (TK, TQ), 1)
    adm_diag = (row_in_tile // CHUNK) <= (lane_in_tile // CHUNK)
    heads = [slice(h * TQ, (h + 1) * TQ) for h in range(N_HEADS_A)]
    tile = lambda kt: pl.ds(pl.multiple_of(kt * TK, TK), TK)
    i16 = jnp.int16
    P16 = 2 * SUBLANES

    def score_block(r0, n, diag):
        rows = pl.ds(r0, n)
        r_all = _dot(ki_ref[rows, :], qit_ref[...])
        sc = jnp.zeros((n, TQ), _F32)
        for sl in heads:
            sc = sc + wit_ref[:, sl] * jnp.maximum(r_all[:, sl], 0.0)
        if diag:
            sc = jnp.where(adm_diag, sc, NEG)
        sc = jnp.where(sc == 0.0, 0.0, sc)
        sc_ref[rows, :] = sc
        bits = pltpu.bitcast(sc, jnp.int32)
        key = bits ^ ((bits >> 31) & 0x7FFFFFFF)
        hi_ref[rows, :] = (key >> 16).astype(i16)
        lo_ref[rows, :] = ((key & 0xFFFF) - 32768).astype(i16)

    def score_body(i, carry):
        score_block(pl.multiple_of(i * TKB, TKB), TKB, False)
        return carry

    lax.fori_loop(0, t // 2, score_body, 0)
    pl.when(t % 2 == 1)(lambda: score_block(pl.multiple_of((t - 1) * TK, TK), TK, False))
    score_block(pl.multiple_of(t * TK, TK), TK, True)

    nkb = (nkt + 1) // 2
    block = lambda i: pl.ds(pl.multiple_of(i * TKB, TKB), TKB)

    @pl.when(nkt % 2 == 1)
    def _():
        pad = jnp.full((TK, TQ), -32768, i16)
        hi_ref[tile(nkt), :] = pad
        lo_ref[tile(nkt), :] = pad

    def count16(ref, pred):
        def body(i, accs):
            blk = ref[block(i), :]
            accs = list(accs)
            for g in range(TKB // P16):
                accs[g % 4] = accs[g % 4] + pred(blk[g * P16:(g + 1) * P16])
            return tuple(accs)
        z = jnp.zeros((P16, TQ), i16)
        a = [x.astype(jnp.int32) for x in lax.fori_loop(0, nkb, body, (z, z, z, z))]
        s = (a[0] + a[1]) + (a[2] + a[3])
        return _allsum_sublanes(s[:SUBLANES] + s[SUBLANES:])

    def rows16(x):
        return jnp.concatenate([x, x], axis=0).astype(i16)

    def count_ge16(ref, cand):
        c16 = rows16(cand)
        return count16(ref, lambda b: jnp.where(b >= c16, i16(1), i16(0)))

    def search16(ref, need, c_base):
        def step(i, carry):
            cur, cnt = carry
            cand = cur + lax.shift_left(jnp.int32(1), 15 - i)
            c = count_ge16(ref, cand)
            ok = c >= need
            return jnp.where(ok, cand, cur), jnp.where(ok, c, cnt)
        return lax.fori_loop(0, 16, step, (jnp.full((SUBLANES, TQ), -32768, jnp.int32), c_base))

    zeros8 = jnp.zeros((SUBLANES, TQ), jnp.int32)
    t_hi, c_hi = search16(hi_ref, zeros8 + n_sel, zeros8 + n_keys)
    c_gt = jnp.where(t_hi >= 32767, 0, count_ge16(hi_ref, jnp.minimum(t_hi + 1, 32767)))
    t_hi16 = rows16(t_hi)

    def narrow_body(i, carry):
        rows = block(i)
        hi, lo = hi_ref[rows, :], lo_ref[rows, :]
        lo_ref[rows, :] = jnp.concatenate(
            [jnp.where(hi[g * P16:(g + 1) * P16] == t_hi16, lo[g * P16:(g + 1) * P16], i16(-32768))
             for g in range(TKB // P16)], axis=0)
        return carry

    lax.fori_loop(0, nkb, narrow_body, 0)
    t_lo, c_lo = search16(lo_ref, n_sel - c_gt, c_hi - c_gt)
    cge = c_gt + c_lo
    thr = _key_to_f32((t_hi << 16) | ((t_lo + 32768) & 0xFFFF))
    thr_row = thr[0:1, :]

    def count32(pred):
        def body(kt, accs):
            r0 = pl.multiple_of(kt * TK, TK)
            blk = sc_ref[pl.ds(r0, TK), :]
            accs = list(accs)
            for g in range(TK // SUBLANES):
                accs[g % 4] = accs[g % 4] + pred(blk[g * SUBLANES:(g + 1) * SUBLANES], r0 + g * SUBLANES)
            return tuple(accs)
        a = lax.fori_loop(0, nkt, body, (zeros8, zeros8, zeros8, zeros8))
        return _allsum_sublanes((a[0] + a[1]) + (a[2] + a[3]))

    sub = lax.broadcasted_iota(jnp.int32, (SUBLANES, TQ), 0)
    jmax_ref[...] = jnp.full_like(jmax_ref, 2 ** 30)
    has_ties = jnp.max(cge) > n_sel

    @pl.when(has_ties)
    def _():
        need = n_sel - count32(lambda blk, r: jnp.where(blk > thr, 1, 0))

        def step(i, j):
            cand = j + lax.shift_left(jnp.int32(1), idx_bits - 1 - i)
            c = count32(lambda blk, r: jnp.where(blk == thr, jnp.where(r + sub < cand, 1, 0), 0))
            return jnp.where(c < need, cand, j)

        jmax_ref[...] = lax.fori_loop(0, idx_bits, step, zeros8)

    def bias_tile(kt, general, diag):
        rows = tile(kt)
        blk = sc_ref[rows, :]
        if general:
            krow = kt * TK + row_in_tile
            pick = jnp.where(blk > thr_row, 1,
                             jnp.where(blk == thr_row, jnp.where(krow <= jmax_ref[0:1, :], 1, 0), 0))
            if diag:
                pick = jnp.where(adm_diag, pick, 0)
            sc_ref[rows, :] = jnp.where(pick > 0, 0.0, NEG)
        else:
            sc_ref[rows, :] = jnp.where(blk >= thr_row, 0.0, NEG)

    def bias_loop(general):
        def body(kt, carry):
            bias_tile(kt, general, False)
            return carry
        lax.fori_loop(0, t, body, 0)

    pl.when(has_ties)(lambda: bias_loop(True))
    pl.when(jnp.logical_not(has_ties))(lambda: bias_loop(False))
    bias_tile(t, True, True)

    acc_ref[...] = jnp.zeros_like(acc_ref)
    m_ref[...] = jnp.full_like(m_ref, NEG)

    def logits_stage(kt, buf):
        rows = tile(kt)
        bias = sc_ref[rows, :]
        ka = ka_ref[rows, :]
        tops = []
        for sl in heads:
            x = _dot(ka, qat_ref[:, sl]) + bias
            lg_ref[buf, :, sl] = x
            tops.append(jnp.max(x, axis=0, keepdims=True))
        return jnp.concatenate(tops, axis=0)

    def softmax_stage(kt, buf, top):
        vt = vt_ref[:, tile(kt)]
        m_old = m_ref[...]
        m_new = jnp.maximum(m_old, top)
        m_ref[...] = m_new
        alpha = jnp.exp2(m_old - m_new)
        for h, sl in enumerate(heads):
            p_ref[:, sl] = jnp.exp2(lg_ref[buf, :, sl] - m_new[h:h + 1, :]).astype(_BF16)
            acc_ref[:, sl] = acc_ref[:, sl] * alpha[h:h + 1, :] + _dot(vt, p_ref[:, sl])

    def attn_body(i, top0):
        top1 = logits_stage(2 * i + 1, 1)
        softmax_stage(2 * i, 0, top0)
        top0 = logits_stage(jnp.minimum(2 * i + 2, t), 0)
        softmax_stage(2 * i + 1, 1, top1)
        return top0

    top_last = lax.fori_loop(0, nkt // 2, attn_body, logits_stage(0, 0))
    pl.when(nkt % 2 == 1)(lambda: softmax_stage(t, 0, top_last))

    for h, sl in enumerate(heads):
        o = acc_ref[:DH_A, sl] / acc_ref[DH_A:DH_A + 1, sl]
        out_ref[:, h * DH_A:(h + 1) * DH_A] = o.T.astype(_BF16)


def _dsa(qat, qit, wit, ka, ki, vt, B, S):
    nt = S // TQ
    n_sel = min(TOPK_MAX, S // 4)
    idx_bits = max(1, int(np.ceil(np.log2(S))))
    kern = functools.partial(_dsa_kernel, n_sel=n_sel, idx_bits=idx_bits)
    return pl.pallas_call(
        kern,
        out_shape=jax.ShapeDtypeStruct((B * S, W_A), _BF16),
        grid=(B, nt),
        in_specs=[
            pl.BlockSpec((None, DH_IDX, HQ), lambda b, t: (b, 0, t)),
            pl.BlockSpec((None, 1, HQ), lambda b, t: (b, 0, t)),
            pl.BlockSpec((None, DH_A, HQ), lambda b, t: (b, 0, t)),
            pl.BlockSpec((None, S, DH_IDX), lambda b, t: (b, 0, 0)),
            pl.BlockSpec((None, S, DH_A), lambda b, t: (b, 0, 0)),
            pl.BlockSpec((None, VT_ROWS, S), lambda b, t: (b, 0, 0)),
        ],
        out_specs=pl.BlockSpec((TQ, W_A), lambda b, t: (b * nt + t, 0)),
        scratch_shapes=[
            pltpu.VMEM((S, TQ), _F32),
            pltpu.VMEM((S, TQ), jnp.int16),
            pltpu.VMEM((S, TQ), jnp.int16),
            pltpu.VMEM((2, TK, HQ), _F32),
            pltpu.VMEM((TK, HQ), _BF16),
            pltpu.VMEM((VT_ROWS, HQ), _F32),
            pltpu.VMEM((N_HEADS_A, TQ), _F32),
            pltpu.VMEM((SUBLANES, TQ), jnp.int32),
        ],
        compiler_params=_params("parallel", "arbitrary"),
        name="dsa",
    )(qit, wit, qat, ki, ka, vt)


def _merge_kernel(hm_ref, ha_ref, gm_ref, ga_ref, x_ref, wbm_ref, wba_ref, wo_ref, g_ref, out_ref):
    ym = _dot(hm_ref[...], wbm_ref[...])
    ya = _dot(ha_ref[...], wba_ref[...])
    y = jax.nn.sigmoid(gm_ref[...].astype(_F32)) * ym + jax.nn.sigmoid(ga_ref[...].astype(_F32)) * ya
    z = _dot(y.astype(_BF16), wo_ref[...])
    zn = z * lax.rsqrt(jnp.mean(z * z, axis=-1, keepdims=True) + EPS) * g_ref[...]
    out_ref[...] = x_ref[...] + zn


def _merge(hm, ha, p16, x2, wbm, wba, wo, g):
    n = x2.shape[0]
    tm = 512
    rows = lambda c: pl.BlockSpec((tm, D_MODEL), lambda i: (i, c))
    return pl.pallas_call(
        _merge_kernel,
        out_shape=jax.ShapeDtypeStruct((n, D_MODEL), _F32),
        grid=(n // tm,),
        in_specs=[rows(0), rows(0), rows(OFF_GM // D_MODEL), rows(OFF_GA // D_MODEL), rows(0),
                  _resident((W_M, D_MODEL)), _resident((W_A, D_MODEL)), _resident((D_MODEL, D_MODEL)),
                  _resident((1, D_MODEL))],
        out_specs=rows(0),
        compiler_params=_params("parallel"),
        name="merge",
    )(hm, ha, p16, p16, x2, wbm, wba, wo, g)


def _ffn_kernel(x_ref, gpre_ref, wg_ref, wu_ref, wd_ref, gpost_ref, out_ref):
    x = x_ref[...]
    f = (x * lax.rsqrt(jnp.mean(x * x, axis=-1, keepdims=True) + EPS) * gpre_ref[...]).astype(_BF16)
    a = _dot(f, wg_ref[...])
    u = _dot(f, wu_ref[...])
    act = (a * jax.nn.sigmoid(a) * u).astype(_BF16)
    z = _dot(act, wd_ref[...])
    out_ref[...] = x + z * lax.rsqrt(jnp.mean(z * z, axis=-1, keepdims=True) + EPS) * gpost_ref[...]


def _ffn(x2, gpre, wg, wu, wd, gpost):
    n = x2.shape[0]
    tm = 512
    rows = pl.BlockSpec((tm, D_MODEL), lambda i: (i, 0))
    return pl.pallas_call(
        _ffn_kernel,
        out_shape=jax.ShapeDtypeStruct((n, D_MODEL), _F32),
        grid=(n // tm,),
        in_specs=[rows, _resident((1, D_MODEL)), _resident((D_MODEL, D_FF)), _resident((D_MODEL, D_FF)),
                  _resident((D_FF, D_MODEL)), _resident((1, D_MODEL))],
        out_specs=rows,
        compiler_params=_params("parallel"),
        name="ffn",
    )(x2, gpre, wg, wu, wd, gpost)


def _pack_in_proj(w, b):
    o = np.cumsum((0, W_M, W_M, W_M, W_M, N_HEADS_M, N_HEADS_M, W_A, DH_A, DH_A,
                   W_IDX, N_HEADS_IDX, DH_IDX, D_MODEL, D_MODEL))
    mq, mk, mv, mo, mi, mf, aq, ak, av, iq, iw, ik, gm, ga = [slice(o[i], o[i + 1]) for i in range(14)]
    order16 = (mq, mk, mv, mo, aq, gm, ga, iq, ak, av)
    order32 = (ik, iw, mi, mf)

    def cat(arr, order, width):
        parts = [arr[..., s] for s in order]
        used = sum(p.shape[-1] for p in parts)
        parts.append(jnp.zeros(arr.shape[:-1] + (width - used,), arr.dtype))
        return jnp.concatenate(parts, axis=-1)

    return (cat(w, order16, N16).astype(_BF16), cat(b, order16, N16)[None, :],
            cat(w, order32, N32).astype(_BF16), cat(b, order32, N32)[None, :])


def _rope_tables(S):
    def tab(dim):
        inv = ROPE_THETA ** (-jnp.arange(dim // 2, dtype=_F32) / (dim // 2))
        ang = jnp.arange(S, dtype=_F32)[:, None] * inv[None, :]
        c, s = jnp.cos(ang), jnp.sin(ang)
        return jnp.concatenate([c, c], axis=-1), jnp.concatenate([-s, s], axis=-1)

    ca, sa = tab(DH_A)
    ci, si = tab(DH_IDX)
    return (ca.T, sa.T, ci.T, si.T, ca, sa, ci, si)


def kernel(x, norm_mix_pre, norm_mix_post, norm_ffn_pre, norm_ffn_post, w_in, b_in, conv_qk,
           norm_mlstm_head, norm_idx_k, w_branch_mlstm, w_branch_attn, w_out,
           w_ffn_gate, w_ffn_up, w_ffn_down):
    B, S, _ = x.shape
    assert S % ML == 0 and S % TKB == 0 and TQ == TK and (B * S) % 1024 == 0
    depth = w_in.shape[0]
    tabs = _rope_tables(S)
    x2 = x.reshape(B * S, D_MODEL)
    for l in range(depth):
        w16, b16, w32, b32 = _pack_in_proj(w_in[l], b_in[l])
        p16, p32 = _in_proj(x2, norm_mix_pre[l][None, :], w16, b16, w32, b32)
        hm = _mlstm(p16, p32, conv_qk[l], norm_mlstm_head[l][None, :], B, S)
        qat, qit, wit, ka, ki, vt = _dsa_prep(p16, p32, norm_idx_k[l][None, :], tabs, B, S)
        ha = _dsa(qat, qit, wit, ka, ki, vt, B, S)
        x2 = _merge(hm, ha, p16, x2, w_branch_mlstm[l].astype(_BF16), w_branch_attn[l].astype(_BF16),
                    w_out[l].astype(_BF16), norm_mix_post[l][None, :])
        x2 = _ffn(x2, norm_ffn_pre[l][None, :], w_ffn_gate[l].astype(_BF16), w_ffn_up[l].astype(_BF16),
                  w_ffn_down[l].astype(_BF16), norm_ffn_post[l][None, :])
    return x2.reshape(B, S, D_MODEL)
```

```python
import functools

import jax
import jax.numpy as jnp
import numpy as np
from jax import lax
from jax.experimental import pallas as pl
from jax.experimental.pallas import tpu as pltpu

D_MODEL = 1024
CHUNK = 64
N_HEADS_M = 4
DH_M = 256
W_M = N_HEADS_M * DH_M
CONV_W = 4
N_HEADS_A = 8
DH_A = 128
W_A = N_HEADS_A * DH_A
N_HEADS_IDX = 8
DH_IDX = 64
W_IDX = N_HEADS_IDX * DH_IDX
IDX_W_SCALE = (N_HEADS_IDX ** -0.5) * (DH_IDX ** -0.5)
TOPK_MAX = 256
D_FF = 2816
ROPE_THETA = 10000.0
EPS = 1e-6
NEG = -1e30

LANES = 128
SUBLANES = 8
VMEM_LIMIT_BYTES = 56 * 1024 * 1024

OFF_MQ, OFF_MK, OFF_MV, OFF_MO = 0, 1024, 2048, 3072
OFF_AQ, OFF_GM, OFF_GA = 4096, 5120, 6144
OFF_IQ, OFF_AK, OFF_AV = 7168, 7680, 7808
N16 = 8192
C_IK, C_IW, C_MI, C_MF = 0, 64, 72, 76
N32 = 128

ML = 256
TQ = 256
TK = 256
TKB = 2 * TK
HQ = N_HEADS_A * TQ
VT_ROWS = DH_A + SUBLANES
INT_MIN = -2 ** 31
LOG2E = 1.4426950408889634
QK_SCALE2 = (DH_A ** -0.5) * LOG2E

_F32 = jnp.float32
_BF16 = jnp.bfloat16


def _dot(a, b):
    return jnp.dot(a, b, preferred_element_type=_F32)


def _dot_nt(a, b):
    return lax.dot_general(a, b, (((1,), (1,)), ((), ())), preferred_element_type=_F32)


def _dot_tn(a, b):
    return lax.dot_general(a, b, (((0,), (0,)), ((), ())), preferred_element_type=_F32)


def _params(*sem):
    return pltpu.CompilerParams(dimension_semantics=sem, vmem_limit_bytes=VMEM_LIMIT_BYTES)


def _resident(shape):
    n = len(shape)
    return pl.BlockSpec(shape, lambda *_: (0,) * n, pipeline_mode=pl.Buffered(1))


def _in_proj_kernel(x_ref, g_ref, w_ref, b_ref, ws_ref, bs_ref, o16_ref, o32_ref, h_ref):
    @pl.when(pl.program_id(1) == 0)
    def _():
        x = x_ref[...]
        y = x * lax.rsqrt(jnp.mean(x * x, axis=-1, keepdims=True) + EPS)
        hb = (y * g_ref[...]).astype(_BF16)
        h_ref[...] = hb
        o32_ref[...] = _dot(hb, ws_ref[...]) + bs_ref[...]

    o16_ref[...] = (_dot(h_ref[...], w_ref[...]) + b_ref[...]).astype(_BF16)


def _in_proj(x2, g, w16, b16, w32, b32):
    n = x2.shape[0]
    tm, tn = 1024, 2048
    return pl.pallas_call(
        _in_proj_kernel,
        out_shape=(jax.ShapeDtypeStruct((n, N16), _BF16), jax.ShapeDtypeStruct((n, N32), _F32)),
        grid=(n // tm, N16 // tn),
        in_specs=[
            pl.BlockSpec((tm, D_MODEL), lambda i, j: (i, 0)),
            pl.BlockSpec((1, D_MODEL), lambda i, j: (0, 0)),
            pl.BlockSpec((D_MODEL, tn), lambda i, j: (0, j)),
            pl.BlockSpec((1, tn), lambda i, j: (0, j)),
            pl.BlockSpec((D_MODEL, N32), lambda i, j: (0, 0)),
            pl.BlockSpec((1, N32), lambda i, j: (0, 0)),
        ],
        out_specs=(
            pl.BlockSpec((tm, tn), lambda i, j: (i, j)),
            pl.BlockSpec((tm, N32), lambda i, j: (i, 0)),
        ),
        scratch_shapes=[pltpu.VMEM((tm, D_MODEL), _BF16)],
        compiler_params=_params("parallel", "arbitrary"),
        name="in_proj",
    )(x2, g, w16, b16, w32, b32)


def _log_sigmoid(x):
    return jnp.minimum(x, 0.0) - jnp.log(1.0 + jnp.exp(-jnp.abs(x)))


def _cumsum_rows(x):
    n = x.shape[0]
    row = lax.broadcasted_iota(jnp.int32, x.shape, 0)
    k = 1
    while k < n:
        x = x + jnp.where(row >= k, pltpu.roll(x, k, axis=0), 0.0)
        k *= 2
    return x


def _mlstm_kernel(qk_ref, v_ref, o_ref, g32_ref, cw_ref, hn_ref, out_ref,
                  ct_ref, n_ref, m_ref, tail_ref):
    @pl.when(pl.program_id(1) == 0)
    def _():
        ct_ref[...] = jnp.zeros_like(ct_ref)
        n_ref[...] = jnp.zeros_like(n_ref)
        m_ref[...] = jnp.zeros_like(m_ref)
        tail_ref[...] = jnp.zeros_like(tail_ref)

    L = ML
    qk_raw = qk_ref[...].astype(_F32)
    ext = jnp.concatenate([tail_ref[...], qk_raw], axis=0)
    tail_ref[...] = qk_raw[L - SUBLANES:, :]
    cw = cw_ref[...]
    conv = ext[SUBLANES:, :] * cw[CONV_W - 1:CONV_W, :]
    for j in range(1, CONV_W):
        conv = conv + pltpu.roll(ext, j, axis=0)[SUBLANES:, :] * cw[CONV_W - 1 - j:CONV_W - j, :]
    half = 0.5 * conv
    qk = half + half * jnp.tanh(half)

    g32 = g32_ref[...]
    bc = _cumsum_rows(_log_sigmoid(g32))
    g32_t = g32.T
    bc_t = bc.T
    row = lax.broadcasted_iota(jnp.int32, (L, L), 0)
    col = lax.broadcasted_iota(jnp.int32, (L, L), 1)
    causal = col <= row

    for h in range(N_HEADS_M):
        sl = slice(h * DH_M, (h + 1) * DH_M)
        q = qk[:, sl].astype(_BF16)
        k = (qk[:, W_M + h * DH_M:W_M + (h + 1) * DH_M] * (DH_M ** -0.5)).astype(_BF16)
        v = v_ref[:, sl]
        b_c = bc[:, C_MF + h:C_MF + h + 1]
        i_c = g32[:, C_MI + h:C_MI + h + 1]
        b_r = bc_t[C_MF + h:C_MF + h + 1, :]
        i_r = g32_t[C_MI + h:C_MI + h + 1, :]
        m_prev = m_ref[h]

        dlog = jnp.where(causal, b_c - b_r + i_r, NEG)
        inter = b_c + m_prev
        mj = jnp.maximum(inter, jnp.max(dlog, axis=-1, keepdims=True))
        dw = jnp.exp(dlog - mj)
        iw = jnp.exp(inter - mj)
        s = _dot_nt(q, k) * dw
        num = iw * _dot(q, ct_ref[h].astype(_BF16)) + _dot(s.astype(_BF16), v)
        qn = jnp.sum(q.astype(_F32) * n_ref[h], axis=-1, keepdims=True)
        den = iw * qn + jnp.sum(s, axis=-1, keepdims=True)
        hh = num / jnp.maximum(jnp.abs(den), jnp.exp(-mj))

        b_last = b_c[L - 1:L, :]
        gg = b_last - b_c + i_c
        m_new = jnp.maximum(b_last + m_prev, jnp.max(gg, axis=0, keepdims=True))
        decay = jnp.exp(b_last + m_prev - m_new)
        w = jnp.exp(gg - m_new)
        kf = k.astype(_F32)
        wv = (w * v.astype(_F32)).astype(_BF16)
        ct_ref[h] = decay * ct_ref[h] + _dot_tn(k, wv)
        n_ref[h] = decay * n_ref[h] + jnp.sum(w * kf, axis=0, keepdims=True)
        m_ref[h] = m_new

        y = hh * lax.rsqrt(jnp.mean(hh * hh, axis=-1, keepdims=True) + EPS) * hn_ref[:, sl]
        gate = jax.nn.sigmoid(o_ref[:, sl].astype(_F32))
        out_ref[:, sl] = (y * gate).astype(_BF16)


def _mlstm(p16, p32, conv_qk, norm_head, B, S):
    n = B * S
    nc = S // ML
    row = lambda b, c: b * nc + c
    return pl.pallas_call(
        _mlstm_kernel,
        out_shape=jax.ShapeDtypeStruct((n, W_M), _BF16),
        grid=(B, nc),
        in_specs=[
            pl.BlockSpec((ML, 2 * W_M), lambda b, c: (row(b, c), 0)),
            pl.BlockSpec((ML, W_M), lambda b, c: (row(b, c), OFF_MV // W_M)),
            pl.BlockSpec((ML, W_M), lambda b, c: (row(b, c), OFF_MO // W_M)),
            pl.BlockSpec((ML, N32), lambda b, c: (row(b, c), 0)),
            pl.BlockSpec((CONV_W, 2 * W_M), lambda b, c: (0, 0)),
            pl.BlockSpec((1, W_M), lambda b, c: (0, 0)),
        ],
        out_specs=pl.BlockSpec((ML, W_M), lambda b, c: (row(b, c), 0)),
        scratch_shapes=[
            pltpu.VMEM((N_HEADS_M, DH_M, DH_M), _F32),
            pltpu.VMEM((N_HEADS_M, 1, DH_M), _F32),
            pltpu.VMEM((N_HEADS_M, 1, 1), _F32),
            pltpu.VMEM((SUBLANES, 2 * W_M), _F32),
        ],
        compiler_params=_params("parallel", "arbitrary"),
        name="mlstm",
    )(p16, p16, p16, p32, conv_qk, norm_head)


def _rot_rows(x, half):
    return jnp.concatenate([x[half:], x[:half]], axis=0)


def _dsa_prep_kernel(aq_ref, iq_ref, ak_ref, av_ref, g32_ref, gk_ref,
                     cat_ref, sat_ref, cit_ref, sit_ref, ca_ref, sa_ref, ci_ref, si_ref,
                     qat_ref, qit_ref, wit_ref, ka_ref, ki_ref, vt_ref):
    cat, sat = cat_ref[...], sat_ref[...]
    aq_t = aq_ref[...].astype(_F32).T
    for h in range(N_HEADS_A):
        x = aq_t[h * DH_A:(h + 1) * DH_A]
        qat_ref[:, h * TQ:(h + 1) * TQ] = ((x * cat + _rot_rows(x, DH_A // 2) * sat) * QK_SCALE2).astype(_BF16)

    cit, sit = cit_ref[...], sit_ref[...]
    iq_t = iq_ref[...].astype(_F32).T
    for h in range(N_HEADS_IDX):
        x = iq_t[h * DH_IDX:(h + 1) * DH_IDX]
        qit_ref[:, h * TQ:(h + 1) * TQ] = (x * cit + _rot_rows(x, DH_IDX // 2) * sit).astype(_BF16)

    g32 = g32_ref[...]
    w_t = g32.T[C_IW:C_IW + N_HEADS_IDX, :] * IDX_W_SCALE
    for h in range(N_HEADS_IDX):
        wit_ref[:, h * TQ:(h + 1) * TQ] = w_t[h:h + 1, :]

    ak = ak_ref[...].astype(_F32)
    ka_ref[...] = (ak * ca_ref[...] + pltpu.roll(ak, DH_A // 2, axis=1) * sa_ref[...]).astype(_BF16)

    ik = g32[:, C_IK:C_IK + DH_IDX]
    xc = ik - jnp.mean(ik, axis=-1, keepdims=True)
    ln = xc * lax.rsqrt(jnp.mean(xc * xc, axis=-1, keepdims=True) + EPS) * gk_ref[...]
    half = DH_IDX // 2
    ln_rot = jnp.concatenate([ln[:, half:], ln[:, :half]], axis=1)
    ki_ref[...] = (ln * ci_ref[...] + ln_rot * si_ref[...]).astype(_BF16)

    vt_ref[:DH_A, :] = av_ref[...].astype(_F32).T.astype(_BF16)
    vt_ref[DH_A:, :] = jnp.ones((VT_ROWS - DH_A, TQ), _BF16)


def _dsa_prep(p16, p32, norm_idx_k, tabs, B, S):
    nt = S // TQ
    row = lambda b, t: b * nt + t
    cat, sat, cit, sit, ca, sa, ci, si = tabs
    return pl.pallas_call(
        _dsa_prep_kernel,
        out_shape=(
            jax.ShapeDtypeStruct((B, DH_A, nt * HQ), _BF16),
            jax.ShapeDtypeStruct((B, DH_IDX, nt * HQ), _BF16),
            jax.ShapeDtypeStruct((B, 1, nt * HQ), _F32),
            jax.ShapeDtypeStruct((B, S, DH_A), _BF16),
            jax.ShapeDtypeStruct((B, S, DH_IDX), _BF16),
            jax.ShapeDtypeStruct((B, VT_ROWS, S), _BF16),
        ),
        grid=(B, nt),
        in_specs=[
            pl.BlockSpec((TQ, W_A), lambda b, t: (row(b, t), OFF_AQ // W_A)),
            pl.BlockSpec((TQ, W_IDX), lambda b, t: (row(b, t), OFF_IQ // W_IDX)),
            pl.BlockSpec((TQ, DH_A), lambda b, t: (row(b, t), OFF_AK // DH_A)),
            pl.BlockSpec((TQ, DH_A), lambda b, t: (row(b, t), OFF_AV // DH_A)),
            pl.BlockSpec((TQ, N32), lambda b, t: (row(b, t), 0)),
            pl.BlockSpec((1, DH_IDX), lambda b, t: (0, 0)),
            pl.BlockSpec((DH_A, TQ), lambda b, t: (0, t)),
            pl.BlockSpec((DH_A, TQ), lambda b, t: (0, t)),
            pl.BlockSpec((DH_IDX, TQ), lambda b, t: (0, t)),
            pl.BlockSpec((DH_IDX, TQ), lambda b, t: (0, t)),
            pl.BlockSpec((TQ, DH_A), lambda b, t: (t, 0)),
            pl.BlockSpec((TQ, DH_A), lambda b, t: (t, 0)),
            pl.BlockSpec((TQ, DH_IDX), lambda b, t: (t, 0)),
            pl.BlockSpec((TQ, DH_IDX), lambda b, t: (t, 0)),
        ],
        out_specs=(
            pl.BlockSpec((None, DH_A, HQ), lambda b, t: (b, 0, t)),
            pl.BlockSpec((None, DH_IDX, HQ), lambda b, t: (b, 0, t)),
            pl.BlockSpec((None, 1, HQ), lambda b, t: (b, 0, t)),
            pl.BlockSpec((None, TQ, DH_A), lambda b, t: (b, t, 0)),
            pl.BlockSpec((None, TQ, DH_IDX), lambda b, t: (b, t, 0)),
            pl.BlockSpec((None, VT_ROWS, TQ), lambda b, t: (b, 0, t)),
        ),
        compiler_params=_params("parallel", "parallel"),
        name="dsa_prep",
    )(p16, p16, p16, p16, p32, norm_idx_k, cat, sat, cit, sit, ca, sa, ci, si)


def _allsum_sublanes(x):
    for k in (4, 2, 1):
        x = x + pltpu.roll(x, k, axis=0)
    return x


def _key_to_f32(key):
    return pltpu.bitcast(key ^ ((key >> 31) & 0x7FFFFFFF), _F32)


def _dsa_kernel(qit_ref, wit_ref, qat_ref, ki_ref, ka_ref, vt_ref, out_ref,
                sc_ref, hi_ref, lo_ref, lg_ref, acc_ref, m_ref, jmax_ref,
                *, n_sel, idx_bits):
    t = pl.program_id(1)
    nkt = t + 1
    n_keys = nkt * TK
    row_in_tile = lax.broadcasted_iota(jnp.int32, (TK, TQ), 0)
    lane_in_tile = lax.broadcasted_iota(jnp.int32, (TK, TQ), 1)
    adm_diag = (row_in_tile // CHUNK) <= (lane_in_tile // CHUNK)
    heads = [slice(h * TQ, (h + 1) * TQ) for h in range(N_HEADS_A)]
    tile = lambda kt: pl.ds(pl.multiple_of(kt * TK, TK), TK)
    i16 = jnp.int16
    P16 = 2 * SUBLANES

    def score_block(r0, n, diag):
        rows = pl.ds(r0, n)
        r_all = _dot(ki_ref[rows, :], qit_ref[...])
        sc = jnp.zeros((n, TQ), _F32)
        for sl in heads:
            sc = sc + wit_ref[:, sl] * jnp.maximum(r_all[:, sl], 0.0)
        if diag:
            sc = jnp.where(adm_diag, sc, NEG)
        sc = jnp.where(sc == 0.0, 0.0, sc)
        sc_ref[rows, :] = sc
        bits = pltpu.bitcast(sc, jnp.int32)
        key = bits ^ ((bits >> 31) & 0x7FFFFFFF)
        hi_ref[rows, :] = (key >> 16).astype(i16)
        lo_ref[rows, :] = ((key & 0xFFFF) - 32768).astype(i16)

    def score_body(i, carry):
        score_block(pl.multiple_of(i * TKB, TKB), TKB, False)
        return carry

    lax.fori_loop(0, t // 2, score_body, 0)
    pl.when(t % 2 == 1)(lambda: score_block(pl.multiple_of((t - 1) * TK, TK), TK, False))
    score_block(pl.multiple_of(t * TK, TK), TK, True)

    nkb = (nkt + 1) // 2
    block = lambda i: pl.ds(pl.multiple_of(i * TKB, TKB), TKB)

    @pl.when(nkt % 2 == 1)
    def _():
        pad = jnp.full((TK, TQ), -32768, i16)
        hi_ref[tile(nkt), :] = pad
        lo_ref[tile(nkt), :] = pad

    def count16(ref, pred):
        def body(i, accs):
            blk = ref[block(i), :]
            accs = list(accs)
            for g in range(TKB // P16):
                accs[g % 4] = accs[g % 4] + pred(blk[g * P16:(g + 1) * P16])
            return tuple(accs)
        z = jnp.zeros((P16, TQ), i16)
        a = [x.astype(jnp.int32) for x in lax.fori_loop(0, nkb, body, (z, z, z, z))]
        s = (a[0] + a[1]) + (a[2] + a[3])
        return _allsum_sublanes(s[:SUBLANES] + s[SUBLANES:])

    def rows16(x):
        return jnp.concatenate([x, x], axis=0).astype(i16)

    def count_ge16(ref, cand):
        c16 = rows16(cand)
        return count16(ref, lambda b: jnp.where(b >= c16, i16(1), i16(0)))

    def search16(ref, need, c_base):
        def step(i, carry):
            cur, cnt = carry
            cand = cur + lax.shift_left(jnp.int32(1), 15 - i)
            c = count_ge16(ref, cand)
            ok = c >= need
            return jnp.where(ok, cand, cur), jnp.where(ok, c, cnt)
        return lax.fori_loop(0, 16, step, (jnp.full((SUBLANES, TQ), -32768, jnp.int32), c_base))

    zeros8 = jnp.zeros((SUBLANES, TQ), jnp.int32)
    t_hi, c_hi = search16(hi_ref, zeros8 + n_sel, zeros8 + n_keys)
    c_gt = jnp.where(t_hi >= 32767, 0, count_ge16(hi_ref, jnp.minimum(t_hi + 1, 32767)))
    t_hi16 = rows16(t_hi)

    def narrow_body(i, carry):
        rows = block(i)
        hi, lo = hi_ref[rows, :], lo_ref[rows, :]
        lo_ref[rows, :] = jnp.concatenate(
            [jnp.where(hi[g * P16:(g + 1) * P16] == t_hi16, lo[g * P16:(g + 1) * P16], i16(-32768))
             for g in range(TKB // P16)], axis=0)
        return carry

    lax.fori_loop(0, nkb, narrow_body, 0)
    t_lo, c_lo = search16(lo_ref, n_sel - c_gt, c_hi - c_gt)
    cge = c_gt + c_lo
    thr = _key_to_f32((t_hi << 16) | ((t_lo + 32768) & 0xFFFF))
    thr_row = thr[0:1, :]

    def count32(pred):
        def body(kt, accs):
            r0 = pl.multiple_of(kt * TK, TK)
            blk = sc_ref[pl.ds(r0, TK), :]
            accs = list(accs)
            for g in range(TK // SUBLANES):
                accs[g % 4] = accs[g % 4] + pred(blk[g * SUBLANES:(g + 1) * SUBLANES], r0 + g * SUBLANES)
            return tuple(accs)
        a = lax.fori_loop(0, nkt, body, (zeros8, zeros8, zeros8, zeros8))
        return _allsum_sublanes((a[0] + a[1]) + (a[2] + a[3]))

    sub = lax.broadcasted_iota(jnp.int32, (SUBLANES, TQ), 0)
    jmax_ref[...] = jnp.full_like(jmax_ref, 2 ** 30)
    has_ties = jnp.max(cge) > n_sel

    @pl.when(has_ties)
    def _():
        need = n_sel - count32(lambda blk, r: jnp.where(blk > thr, 1, 0))

        def step(i, j):
            cand = j + lax.shift_left(jnp.int32(1), idx_bits - 1 - i)
            c = count32(lambda blk, r: jnp.where(blk == thr, jnp.where(r + sub < cand, 1, 0), 0))
            return jnp.where(c < need, cand, j)

        jmax_ref[...] = lax.fori_loop(0, idx_bits, step, zeros8)

    def bias_tile(kt, general, diag):
        rows = tile(kt)
        blk = sc_ref[rows, :]
        if general:
            krow = kt * TK + row_in_tile
            pick = jnp.where(blk > thr_row, 1,
                             jnp.where(blk == thr_row, jnp.where(krow <= jmax_ref[0:1, :], 1, 0), 0))
            if diag:
                pick = jnp.where(adm_diag, pick, 0)
            sc_ref[rows, :] = jnp.where(pick > 0, 0.0, NEG)
        else:
            sc_ref[rows, :] = jnp.where(blk >= thr_row, 0.0, NEG)

    def bias_loop(general):
        def body(kt, carry):
            bias_tile(kt, general, False)
            return carry
        lax.fori_loop(0, t, body, 0)

    pl.when(has_ties)(lambda: bias_loop(True))
    pl.when(jnp.logical_not(has_ties))(lambda: bias_loop(False))
    bias_tile(t, True, True)

    acc_ref[...] = jnp.zeros_like(acc_ref)
    m_ref[...] = jnp.full_like(m_ref, NEG)

    def logits_stage(kt, buf):
        rows = tile(kt)
        bias = sc_ref[rows, :]
        ka = ka_ref[rows, :]
        tops = []
        for sl in heads:
            x = _dot(ka, qat_ref[:, sl]) + bias
            lg_ref[buf, :, sl] = x
            tops.append(jnp.max(x, axis=0, keepdims=True))
        return jnp.concatenate(tops, axis=0)

    def softmax_stage(kt, buf, top):
        vt = vt_ref[:, tile(kt)]
        m_old = m_ref[...]
        m_new = jnp.maximum(m_old, top)
        m_ref[...] = m_new
        alpha = jnp.exp2(m_old - m_new)
        for h, sl in enumerate(heads):
            p = jnp.exp2(lg_ref[buf, :, sl] - m_new[h:h + 1, :]).astype(_BF16)
            acc_ref[:, sl] = acc_ref[:, sl] * alpha[h:h + 1, :] + _dot(vt, p)

    def attn_body(i, top0):
        top1 = logits_stage(2 * i + 1, 1)
        softmax_stage(2 * i, 0, top0)
        top0 = logits_stage(jnp.minimum(2 * i + 2, t), 0)
        softmax_stage(2 * i + 1, 1, top1)
        return top0

    top_last = lax.fori_loop(0, nkt // 2, attn_body, logits_stage(0, 0))
    pl.when(nkt % 2 == 1)(lambda: softmax_stage(t, 0, top_last))

    for h, sl in enumerate(heads):
        o = acc_ref[:DH_A, sl] / acc_ref[DH_A:DH_A + 1, sl]
        out_ref[:, h * DH_A:(h + 1) * DH_A] = o.T.astype(_BF16)


def _dsa(qat, qit, wit, ka, ki, vt, B, S):
    nt = S // TQ
    n_sel = min(TOPK_MAX, S // 4)
    idx_bits = max(1, int(np.ceil(np.log2(S))))
    kern = functools.partial(_dsa_kernel, n_sel=n_sel, idx_bits=idx_bits)
    return pl.pallas_call(
        kern,
        out_shape=jax.ShapeDtypeStruct((B * S, W_A), _BF16),
        grid=(B, nt),
        in_specs=[
            pl.BlockSpec((None, DH_IDX, HQ), lambda b, t: (b, 0, t)),
            pl.BlockSpec((None, 1, HQ), lambda b, t: (b, 0, t)),
            pl.BlockSpec((None, DH_A, HQ), lambda b, t: (b, 0, t)),
            pl.BlockSpec((None, S, DH_IDX), lambda b, t: (b, 0, 0)),
            pl.BlockSpec((None, S, DH_A), lambda b, t: (b, 0, 0)),
            pl.BlockSpec((None, VT_ROWS, S), lambda b, t: (b, 0, 0)),
        ],
        out_specs=pl.BlockSpec((TQ, W_A), lambda b, t: (b * nt + t, 0)),
        scratch_shapes=[
            pltpu.VMEM((S, TQ), _F32),
            pltpu.VMEM((S, TQ), jnp.int16),
            pltpu.VMEM((S, TQ), jnp.int16),
            pltpu.VMEM((2, TK, HQ), _F32),
            pltpu.VMEM((VT_ROWS, HQ), _F32),
            pltpu.VMEM((N_HEADS_A, TQ), _F32),
            pltpu.VMEM((SUBLANES, TQ), jnp.int32),
        ],
        compiler_params=_params("parallel", "arbitrary"),
        name="dsa",
    )(qit, wit, qat, ki, ka, vt)


def _merge_kernel(hm_ref, ha_ref, gm_ref, ga_ref, x_ref, wbm_ref, wba_ref, wo_ref, g_ref, out_ref):
    ym = _dot(hm_ref[...], wbm_ref[...])
    ya = _dot(ha_ref[...], wba_ref[...])
    y = jax.nn.sigmoid(gm_ref[...].astype(_F32)) * ym + jax.nn.sigmoid(ga_ref[...].astype(_F32)) * ya
    z = _dot(y.astype(_BF16), wo_ref[...])
    zn = z * lax.rsqrt(jnp.mean(z * z, axis=-1, keepdims=True) + EPS) * g_ref[...]
    out_ref[...] = x_ref[...] + zn


def _merge(hm, ha, p16, x2, wbm, wba, wo, g):
    n = x2.shape[0]
    tm = 512
    rows = lambda c: pl.BlockSpec((tm, D_MODEL), lambda i: (i, c))
    return pl.pallas_call(
        _merge_kernel,
        out_shape=jax.ShapeDtypeStruct((n, D_MODEL), _F32),
        grid=(n // tm,),
        in_specs=[rows(0), rows(0), rows(OFF_GM // D_MODEL), rows(OFF_GA // D_MODEL), rows(0),
                  _resident((W_M, D_MODEL)), _resident((W_A, D_MODEL)), _resident((D_MODEL, D_MODEL)),
                  _resident((1, D_MODEL))],
        out_specs=rows(0),
        compiler_params=_params("parallel"),
        name="merge",
    )(hm, ha, p16, p16, x2, wbm, wba, wo, g)


def _ffn_kernel(x_ref, gpre_ref, wg_ref, wu_ref, wd_ref, gpost_ref, out_ref):
    x = x_ref[...]
    f = (x * lax.rsqrt(jnp.mean(x * x, axis=-1, keepdims=True) + EPS) * gpre_ref[...]).astype(_BF16)
    a = _dot(f, wg_ref[...])
    u = _dot(f, wu_ref[...])
    act = (a * jax.nn.sigmoid(a) * u).astype(_BF16)
    z = _dot(act, wd_ref[...])
    out_ref[...] = x + z * lax.rsqrt(jnp.mean(z * z, axis=-1, keepdims=True) + EPS) * gpost_ref[...]


def _ffn(x2, gpre, wg, wu, wd, gpost):
    n = x2.shape[0]
    tm = 512
    rows = pl.BlockSpec((tm, D_MODEL), lambda i: (i, 0))
    return pl.pallas_call(
        _ffn_kernel,
        out_shape=jax.ShapeDtypeStruct((n, D_MODEL), _F32),
        grid=(n // tm,),
        in_specs=[rows, _resident((1, D_MODEL)), _resident((D_MODEL, D_FF)), _resident((D_MODEL, D_FF)),
                  _resident((D_FF, D_MODEL)), _resident((1, D_MODEL))],
        out_specs=rows,
        compiler_params=_params("parallel"),
        name="ffn",
    )(x2, gpre, wg, wu, wd, gpost)


def _pack_in_proj(w, b):
    o = np.cumsum((0, W_M, W_M, W_M, W_M, N_HEADS_M, N_HEADS_M, W_A, DH_A, DH_A,
                   W_IDX, N_HEADS_IDX, DH_IDX, D_MODEL, D_MODEL))
    mq, mk, mv, mo, mi, mf, aq, ak, av, iq, iw, ik, gm, ga = [slice(o[i], o[i + 1]) for i in range(14)]
    order16 = (mq, mk, mv, mo, aq, gm, ga, iq, ak, av)
    order32 = (ik, iw, mi, mf)

    def cat(arr, order, width):
        parts = [arr[..., s] for s in order]
        used = sum(p.shape[-1] for p in parts)
        parts.append(jnp.zeros(arr.shape[:-1] + (width - used,), arr.dtype))
        return jnp.concatenate(parts, axis=-1)

    return (cat(w, order16, N16).astype(_BF16), cat(b, order16, N16)[None, :],
            cat(w, order32, N32).astype(_BF16), cat(b, order32, N32)[None, :])


def _rope_tables(S):
    def tab(dim):
        inv = ROPE_THETA ** (-jnp.arange(dim // 2, dtype=_F32) / (dim // 2))
        ang = jnp.arange(S, dtype=_F32)[:, None] * inv[None, :]
        c, s = jnp.cos(ang), jnp.sin(ang)
        return jnp.concatenate([c, c], axis=-1), jnp.concatenate([-s, s], axis=-1)

    ca, sa = tab(DH_A)
    ci, si = tab(DH_IDX)
    return (ca.T, sa.T, ci.T, si.T, ca, sa, ci, si)


def kernel(x, norm_mix_pre, norm_mix_post, norm_ffn_pre, norm_ffn_post, w_in, b_in, conv_qk,
           norm_mlstm_head, norm_idx_k, w_branch_mlstm, w_branch_attn, w_out,
           w_ffn_gate, w_ffn_up, w_ffn_down):
    B, S, _ = x.shape
    assert S % ML == 0 and S % TKB == 0 and TQ == TK and (B * S) % 1024 == 0
    depth = w_in.shape[0]
    tabs = _rope_tables(S)
    x2 = x.reshape(B * S, D_MODEL)
    for l in range(depth):
        w16, b16, w32, b32 = _pack_in_proj(w_in[l], b_in[l])
        p16, p32 = _in_proj(x2, norm_mix_pre[l][None, :], w16, b16, w32, b32)
        hm = _mlstm(p16, p32, conv_qk[l], norm_mlstm_head[l][None, :], B, S)
        qat, qit, wit, ka, ki, vt = _dsa_prep(p16, p32, norm_idx_k[l][None, :], tabs, B, S)
        ha = _dsa(qat, qit, wit, ka, ki, vt, B, S)
        x2 = _merge(hm, ha, p16, x2, w_branch_mlstm[l].astype(_BF16), w_branch_attn[l].astype(_BF16),
                    w_out[l].astype(_BF16), norm_mix_post[l][None, :])
        x2 = _ffn(x2, norm_ffn_pre[l][None, :], w_ffn_gate[l].astype(_BF16), w_ffn_up[l].astype(_BF16),
                  w_ffn_down[l].astype(_BF16), norm_ffn_post[l][None, :])
    return x2.reshape(B, S, D_MODEL)
```

```python
import functools

import jax
import jax.numpy as jnp
import numpy as np
from jax import lax
from jax.experimental import pallas as pl
from jax.experimental.pallas import tpu as pltpu

D_MODEL = 1024
CHUNK = 64
N_HEADS_M = 4
DH_M = 256
W_M = N_HEADS_M * DH_M
CONV_W = 4
N_HEADS_A = 8
DH_A = 128
W_A = N_HEADS_A * DH_A
N_HEADS_IDX = 8
DH_IDX = 64
W_IDX = N_HEADS_IDX * DH_IDX
IDX_W_SCALE = (N_HEADS_IDX ** -0.5) * (DH_IDX ** -0.5)
TOPK_MAX = 256
D_FF = 2816
ROPE_THETA = 10000.0
EPS = 1e-6
NEG = -1e30

LANES = 128
SUBLANES = 8
VMEM_LIMIT_BYTES = 56 * 1024 * 1024

OFF_MQ, OFF_MK, OFF_MV, OFF_MO = 0, 1024, 2048, 3072
OFF_AQ, OFF_GM, OFF_GA = 4096, 5120, 6144
OFF_IQ, OFF_AK, OFF_AV = 7168, 7680, 7808
N16 = 8192
C_IK, C_IW, C_MI, C_MF = 0, 64, 72, 76
N32 = 128

ML = 256
TQ = 256
TK = 256
TKB = 2 * TK
TP = 4 * TQ
HQ = N_HEADS_A * TQ
VT_ROWS = DH_A + SUBLANES
INT_MIN = -2 ** 31
LOG2E = 1.4426950408889634
QK_SCALE2 = (DH_A ** -0.5) * LOG2E

_F32 = jnp.float32
_BF16 = jnp.bfloat16


def _dot(a, b):
    return jnp.dot(a, b, preferred_element_type=_F32)


def _dot_nt(a, b):
    return lax.dot_general(a, b, (((1,), (1,)), ((), ())), preferred_element_type=_F32)


def _dot_tn(a, b):
    return lax.dot_general(a, b, (((0,), (0,)), ((), ())), preferred_element_type=_F32)


def _params(*sem):
    return pltpu.CompilerParams(dimension_semantics=sem, vmem_limit_bytes=VMEM_LIMIT_BYTES)


def _resident(shape):
    n = len(shape)
    return pl.BlockSpec(shape, lambda *_: (0,) * n, pipeline_mode=pl.Buffered(1))


def _in_proj_kernel(x_ref, g_ref, w_ref, b_ref, ws_ref, bs_ref, o16_ref, o32_ref, h_ref):
    @pl.when(pl.program_id(1) == 0)
    def _():
        x = x_ref[...]
        y = x * lax.rsqrt(jnp.mean(x * x, axis=-1, keepdims=True) + EPS)
        hb = (y * g_ref[...]).astype(_BF16)
        h_ref[...] = hb
        o32_ref[...] = _dot(hb, ws_ref[...]) + bs_ref[...]

    o16_ref[...] = (_dot(h_ref[...], w_ref[...]) + b_ref[...]).astype(_BF16)


def _in_proj(x2, g, w16, b16, w32, b32):
    n = x2.shape[0]
    tm, tn = 1024, 2048
    return pl.pallas_call(
        _in_proj_kernel,
        out_shape=(jax.ShapeDtypeStruct((n, N16), _BF16), jax.ShapeDtypeStruct((n, N32), _F32)),
        grid=(n // tm, N16 // tn),
        in_specs=[
            pl.BlockSpec((tm, D_MODEL), lambda i, j: (i, 0)),
            pl.BlockSpec((1, D_MODEL), lambda i, j: (0, 0)),
            pl.BlockSpec((D_MODEL, tn), lambda i, j: (0, j)),
            pl.BlockSpec((1, tn), lambda i, j: (0, j)),
            pl.BlockSpec((D_MODEL, N32), lambda i, j: (0, 0)),
            pl.BlockSpec((1, N32), lambda i, j: (0, 0)),
        ],
        out_specs=(
            pl.BlockSpec((tm, tn), lambda i, j: (i, j)),
            pl.BlockSpec((tm, N32), lambda i, j: (i, 0)),
        ),
        scratch_shapes=[pltpu.VMEM((tm, D_MODEL), _BF16)],
        compiler_params=_params("parallel", "arbitrary"),
        name="in_proj",
    )(x2, g, w16, b16, w32, b32)


def _log_sigmoid(x):
    return jnp.minimum(x, 0.0) - jnp.log(1.0 + jnp.exp(-jnp.abs(x)))


def _cumsum_rows(x):
    n = x.shape[0]
    row = lax.broadcasted_iota(jnp.int32, x.shape, 0)
    k = 1
    while k < n:
        x = x + jnp.where(row >= k, pltpu.roll(x, k, axis=0), 0.0)
        k *= 2
    return x


def _mlstm_kernel(qk_ref, v_ref, o_ref, g32_ref, cw_ref, hn_ref, out_ref,
                  ct_ref, n_ref, m_ref, tail_ref):
    @pl.when(pl.program_id(1) == 0)
    def _():
        ct_ref[...] = jnp.zeros_like(ct_ref)
        n_ref[...] = jnp.zeros_like(n_ref)
        m_ref[...] = jnp.zeros_like(m_ref)
        tail_ref[...] = jnp.zeros_like(tail_ref)

    L = ML
    qk_raw = qk_ref[...].astype(_F32)
    ext = jnp.concatenate([tail_ref[...], qk_raw], axis=0)
    tail_ref[...] = qk_raw[L - SUBLANES:, :]
    cw = cw_ref[...]
    conv = ext[SUBLANES:, :] * cw[CONV_W - 1:CONV_W, :]
    for j in range(1, CONV_W):
        conv = conv + pltpu.roll(ext, j, axis=0)[SUBLANES:, :] * cw[CONV_W - 1 - j:CONV_W - j, :]
    half = 0.5 * conv
    qk = half + half * jnp.tanh(half)

    g32 = g32_ref[...]
    bc = _cumsum_rows(_log_sigmoid(g32))
    g32_t = g32.T
    bc_t = bc.T
    row = lax.broadcasted_iota(jnp.int32, (L, L), 0)
    col = lax.broadcasted_iota(jnp.int32, (L, L), 1)
    causal = col <= row

    for h in range(N_HEADS_M):
        sl = slice(h * DH_M, (h + 1) * DH_M)
        q = qk[:, sl].astype(_BF16)
        k = (qk[:, W_M + h * DH_M:W_M + (h + 1) * DH_M] * (DH_M ** -0.5)).astype(_BF16)
        v = v_ref[:, sl]
        b_c = bc[:, C_MF + h:C_MF + h + 1]
        i_c = g32[:, C_MI + h:C_MI + h + 1]
        b_r = bc_t[C_MF + h:C_MF + h + 1, :]
        i_r = g32_t[C_MI + h:C_MI + h + 1, :]
        m_prev = m_ref[h]

        dlog = jnp.where(causal, b_c - b_r + i_r, NEG)
        inter = b_c + m_prev
        mj = jnp.maximum(inter, jnp.max(dlog, axis=-1, keepdims=True))
        dw = jnp.exp(dlog - mj)
        iw = jnp.exp(inter - mj)
        s = _dot_nt(q, k) * dw
        num = iw * _dot(q, ct_ref[h].astype(_BF16)) + _dot(s.astype(_BF16), v)
        qn = jnp.sum(q.astype(_F32) * n_ref[h], axis=-1, keepdims=True)
        den = iw * qn + jnp.sum(s, axis=-1, keepdims=True)
        hh = num / jnp.maximum(jnp.abs(den), jnp.exp(-mj))

        b_last = b_c[L - 1:L, :]
        gg = b_last - b_c + i_c
        m_new = jnp.maximum(b_last + m_prev, jnp.max(gg, axis=0, keepdims=True))
        decay = jnp.exp(b_last + m_prev - m_new)
        w = jnp.exp(gg - m_new)
        kf = k.astype(_F32)
        wv = (w * v.astype(_F32)).astype(_BF16)
        ct_ref[h] = decay * ct_ref[h] + _dot_tn(k, wv)
        n_ref[h] = decay * n_ref[h] + jnp.sum(w * kf, axis=0, keepdims=True)
        m_ref[h] = m_new

        y = hh * lax.rsqrt(jnp.mean(hh * hh, axis=-1, keepdims=True) + EPS) * hn_ref[:, sl]
        gate = jax.nn.sigmoid(o_ref[:, sl].astype(_F32))
        out_ref[:, sl] = (y * gate).astype(_BF16)


def _mlstm(p16, p32, conv_qk, norm_head, B, S):
    n = B * S
    nc = S // ML
    row = lambda b, c: b * nc + c
    return pl.pallas_call(
        _mlstm_kernel,
        out_shape=jax.ShapeDtypeStruct((n, W_M), _BF16),
        grid=(B, nc),
        in_specs=[
            pl.BlockSpec((ML, 2 * W_M), lambda b, c: (row(b, c), 0)),
            pl.BlockSpec((ML, W_M), lambda b, c: (row(b, c), OFF_MV // W_M)),
            pl.BlockSpec((ML, W_M), lambda b, c: (row(b, c), OFF_MO // W_M)),
            pl.BlockSpec((ML, N32), lambda b, c: (row(b, c), 0)),
            pl.BlockSpec((CONV_W, 2 * W_M), lambda b, c: (0, 0)),
            pl.BlockSpec((1, W_M), lambda b, c: (0, 0)),
        ],
        out_specs=pl.BlockSpec((ML, W_M), lambda b, c: (row(b, c), 0)),
        scratch_shapes=[
            pltpu.VMEM((N_HEADS_M, DH_M, DH_M), _F32),
            pltpu.VMEM((N_HEADS_M, 1, DH_M), _F32),
            pltpu.VMEM((N_HEADS_M, 1, 1), _F32),
            pltpu.VMEM((SUBLANES, 2 * W_M), _F32),
        ],
        compiler_params=_params("parallel", "arbitrary"),
        name="mlstm",
    )(p16, p16, p16, p32, conv_qk, norm_head)


def _rot_rows(x, half):
    return jnp.concatenate([x[half:], x[:half]], axis=0)


def _dsa_prep_kernel(aq_ref, iq_ref, ak_ref, av_ref, g32_ref, gk_ref,
                     cat_ref, sat_ref, cit_ref, sit_ref, ca_ref, sa_ref, ci_ref, si_ref,
                     qat_ref, qit_ref, wit_ref, ka_ref, ki_ref, vt_ref):
    for s in range(TP // TQ):
        rows = slice(s * TQ, (s + 1) * TQ)
        slab = s * HQ
        cat, sat = cat_ref[:, rows], sat_ref[:, rows]
        aq_t = aq_ref[rows, :].astype(_F32).T
        for h in range(N_HEADS_A):
            x = aq_t[h * DH_A:(h + 1) * DH_A]
            qat_ref[:, slab + h * TQ:slab + (h + 1) * TQ] = (
                (x * cat + _rot_rows(x, DH_A // 2) * sat) * QK_SCALE2).astype(_BF16)

        cit, sit = cit_ref[:, rows], sit_ref[:, rows]
        iq_t = iq_ref[rows, :].astype(_F32).T
        for h in range(N_HEADS_IDX):
            x = iq_t[h * DH_IDX:(h + 1) * DH_IDX]
            qit_ref[:, slab + h * TQ:slab + (h + 1) * TQ] = (
                x * cit + _rot_rows(x, DH_IDX // 2) * sit).astype(_BF16)

        w_t = g32_ref[rows, :].T[C_IW:C_IW + N_HEADS_IDX, :] * IDX_W_SCALE
        for h in range(N_HEADS_IDX):
            wit_ref[:, slab + h * TQ:slab + (h + 1) * TQ] = w_t[h:h + 1, :]

    ak = ak_ref[...].astype(_F32)
    ka_ref[...] = (ak * ca_ref[...] + pltpu.roll(ak, DH_A // 2, axis=1) * sa_ref[...]).astype(_BF16)

    ik = g32_ref[:, C_IK:C_IK + DH_IDX]
    xc = ik - jnp.mean(ik, axis=-1, keepdims=True)
    ln = xc * lax.rsqrt(jnp.mean(xc * xc, axis=-1, keepdims=True) + EPS) * gk_ref[...]
    half = DH_IDX // 2
    ln_rot = jnp.concatenate([ln[:, half:], ln[:, :half]], axis=1)
    ki_ref[...] = (ln * ci_ref[...] + ln_rot * si_ref[...]).astype(_BF16)

    vt_ref[:DH_A, :] = av_ref[...].astype(_F32).T.astype(_BF16)
    vt_ref[DH_A:, :] = jnp.ones((VT_ROWS - DH_A, TP), _BF16)


def _dsa_prep(p16, p32, norm_idx_k, tabs, B, S):
    nt = S // TQ
    npt = S // TP
    row = lambda b, t: b * npt + t
    cat, sat, cit, sit, ca, sa, ci, si = tabs
    return pl.pallas_call(
        _dsa_prep_kernel,
        out_shape=(
            jax.ShapeDtypeStruct((B, DH_A, nt * HQ), _BF16),
            jax.ShapeDtypeStruct((B, DH_IDX, nt * HQ), _BF16),
            jax.ShapeDtypeStruct((B, 1, nt * HQ), _F32),
            jax.ShapeDtypeStruct((B, S, DH_A), _BF16),
            jax.ShapeDtypeStruct((B, S, DH_IDX), _BF16),
            jax.ShapeDtypeStruct((B, VT_ROWS, S), _BF16),
        ),
        grid=(B, npt),
        in_specs=[
            pl.BlockSpec((TP, W_A), lambda b, t: (row(b, t), OFF_AQ // W_A)),
            pl.BlockSpec((TP, W_IDX), lambda b, t: (row(b, t), OFF_IQ // W_IDX)),
            pl.BlockSpec((TP, DH_A), lambda b, t: (row(b, t), OFF_AK // DH_A)),
            pl.BlockSpec((TP, DH_A), lambda b, t: (row(b, t), OFF_AV // DH_A)),
            pl.BlockSpec((TP, N32), lambda b, t: (row(b, t), 0)),
            pl.BlockSpec((1, DH_IDX), lambda b, t: (0, 0)),
            pl.BlockSpec((DH_A, TP), lambda b, t: (0, t)),
            pl.BlockSpec((DH_A, TP), lambda b, t: (0, t)),
            pl.BlockSpec((DH_IDX, TP), lambda b, t: (0, t)),
            pl.BlockSpec((DH_IDX, TP), lambda b, t: (0, t)),
            pl.BlockSpec((TP, DH_A), lambda b, t: (t, 0)),
            pl.BlockSpec((TP, DH_A), lambda b, t: (t, 0)),
            pl.BlockSpec((TP, DH_IDX), lambda b, t: (t, 0)),
            pl.BlockSpec((TP, DH_IDX), lambda b, t: (t, 0)),
        ],
        out_specs=(
            pl.BlockSpec((None, DH_A, (TP // TQ) * HQ), lambda b, t: (b, 0, t)),
            pl.BlockSpec((None, DH_IDX, (TP // TQ) * HQ), lambda b, t: (b, 0, t)),
            pl.BlockSpec((None, 1, (TP // TQ) * HQ), lambda b, t: (b, 0, t)),
            pl.BlockSpec((None, TP, DH_A), lambda b, t: (b, t, 0)),
            pl.BlockSpec((None, TP, DH_IDX), lambda b, t: (b, t, 0)),
            pl.BlockSpec((None, VT_ROWS, TP), lambda b, t: (b, 0, t)),
        ),
        compiler_params=_params("parallel", "parallel"),
        name="dsa_prep",
    )(p16, p16, p16, p16, p32, norm_idx_k, cat, sat, cit, sit, ca, sa, ci, si)


def _allsum_sublanes(x):
    for k in (4, 2, 1):
        x = x + pltpu.roll(x, k, axis=0)
    return x


def _key_to_f32(key):
    return pltpu.bitcast(key ^ ((key >> 31) & 0x7FFFFFFF), _F32)


def _dsa_kernel(qit_ref, wit_ref, qat_ref, ki_ref, ka_ref, vt_ref, out_ref,
                sc_ref, hi_ref, lo_ref, lg_ref, acc_ref, m_ref, jmax_ref,
                *, n_sel, idx_bits):
    t = pl.program_id(1)
    nkt = t + 1
    n_keys = nkt * TK
    row_in_tile = lax.broadcasted_iota(jnp.int32, (TK, TQ), 0)
    lane_in_tile = lax.broadcasted_iota(jnp.int32, (TK, TQ), 1)
    adm_diag = (row_in_tile // CHUNK) <= (lane_in_tile // CHUNK)
    heads = [slice(h * TQ, (h + 1) * TQ) for h in range(N_HEADS_A)]
    tile = lambda kt: pl.ds(pl.multiple_of(kt * TK, TK), TK)
    i16 = jnp.int16
    P16 = 2 * SUBLANES

    def score_block(r0, n, diag):
        rows = pl.ds(r0, n)
        r_all = _dot(ki_ref[rows, :], qit_ref[...])
        sc = jnp.zeros((n, TQ), _F32)
        for sl in heads:
            sc = sc + wit_ref[:, sl] * jnp.maximum(r_all[:, sl], 0.0)
        if diag:
            sc = jnp.where(adm_diag, sc, NEG)
        sc = jnp.where(sc == 0.0, 0.0, sc)
        sc_ref[rows, :] = sc
        bits = pltpu.bitcast(sc, jnp.int32)
        key = bits ^ ((bits >> 31) & 0x7FFFFFFF)
        hi_ref[rows, :] = (key >> 16).astype(i16)
        lo_ref[rows, :] = ((key & 0xFFFF) - 32768).astype(i16)

    def score_body(i, carry):
        score_block(pl.multiple_of(i * TKB, TKB), TKB, False)
        return carry

    lax.fori_loop(0, t // 2, score_body, 0)
    pl.when(t % 2 == 1)(lambda: score_block(pl.multiple_of((t - 1) * TK, TK), TK, False))
    score_block(pl.multiple_of(t * TK, TK), TK, True)

    nkb = (nkt + 1) // 2
    block = lambda i: pl.ds(pl.multiple_of(i * TKB, TKB), TKB)

    @pl.when(nkt % 2 == 1)
    def _():
        pad = jnp.full((TK, TQ), -32768, i16)
        hi_ref[tile(nkt), :] = pad
        lo_ref[tile(nkt), :] = pad

    def count16(ref, pred):
        def body(i, accs):
            blk = ref[block(i), :]
            accs = list(accs)
            for g in range(TKB // P16):
                accs[g % 4] = accs[g % 4] + pred(blk[g * P16:(g + 1) * P16])
            return tuple(accs)
        z = jnp.zeros((P16, TQ), i16)
        a = [x.astype(jnp.int32) for x in lax.fori_loop(0, nkb, body, (z, z, z, z))]
        s = (a[0] + a[1]) + (a[2] + a[3])
        return _allsum_sublanes(s[:SUBLANES] + s[SUBLANES:])

    def rows16(x):
        return jnp.concatenate([x, x], axis=0).astype(i16)

    def count_ge16(ref, cand):
        c16 = rows16(cand)
        return count16(ref, lambda b: jnp.where(b >= c16, i16(1), i16(0)))

    def search16(ref, need, c_base):
        def step(i, carry):
            cur, cnt = carry
            cand = cur + lax.shift_left(jnp.int32(1), 15 - i)
            c = count_ge16(ref, cand)
            ok = c >= need
            return jnp.where(ok, cand, cur), jnp.where(ok, c, cnt)
        return lax.fori_loop(0, 16, step, (jnp.full((SUBLANES, TQ), -32768, jnp.int32), c_base))

    zeros8 = jnp.zeros((SUBLANES, TQ), jnp.int32)
    t_hi, c_hi = search16(hi_ref, zeros8 + n_sel, zeros8 + n_keys)
    c_gt = jnp.where(t_hi >= 32767, 0, count_ge16(hi_ref, jnp.minimum(t_hi + 1, 32767)))
    t_hi16 = rows16(t_hi)

    def narrow_body(i, carry):
        rows = block(i)
        hi, lo = hi_ref[rows, :], lo_ref[rows, :]
        lo_ref[rows, :] = jnp.concatenate(
            [jnp.where(hi[g * P16:(g + 1) * P16] == t_hi16, lo[g * P16:(g + 1) * P16], i16(-32768))
             for g in range(TKB // P16)], axis=0)
        return carry

    lax.fori_loop(0, nkb, narrow_body, 0)
    t_lo, c_lo = search16(lo_ref, n_sel - c_gt, c_hi - c_gt)
    cge = c_gt + c_lo
    thr = _key_to_f32((t_hi << 16) | ((t_lo + 32768) & 0xFFFF))
    thr_row = thr[0:1, :]

    def count32(pred):
        def body(kt, accs):
            r0 = pl.multiple_of(kt * TK, TK)
            blk = sc_ref[pl.ds(r0, TK), :]
            accs = list(accs)
            for g in range(TK // SUBLANES):
                accs[g % 4] = accs[g % 4] + pred(blk[g * SUBLANES:(g + 1) * SUBLANES], r0 + g * SUBLANES)
            return tuple(accs)
        a = lax.fori_loop(0, nkt, body, (zeros8, zeros8, zeros8, zeros8))
        return _allsum_sublanes((a[0] + a[1]) + (a[2] + a[3]))

    sub = lax.broadcasted_iota(jnp.int32, (SUBLANES, TQ), 0)
    jmax_ref[...] = jnp.full_like(jmax_ref, 2 ** 30)
    has_ties = jnp.max(cge) > n_sel

    @pl.when(has_ties)
    def _():
        need = n_sel - count32(lambda blk, r: jnp.where(blk > thr, 1, 0))

        def step(i, j):
            cand = j + lax.shift_left(jnp.int32(1), idx_bits - 1 - i)
            c = count32(lambda blk, r: jnp.where(blk == thr, jnp.where(r + sub < cand, 1, 0), 0))
            return jnp.where(c < need, cand, j)

        jmax_ref[...] = lax.fori_loop(0, idx_bits, step, zeros8)

    def bias_tile(kt, general, diag):
        rows = tile(kt)
        blk = sc_ref[rows, :]
        if general:
            krow = kt * TK + row_in_tile
            pick = jnp.where(blk > thr_row, 1,
                             jnp.where(blk == thr_row, jnp.where(krow <= jmax_ref[0:1, :], 1, 0), 0))
            if diag:
                pick = jnp.where(adm_diag, pick, 0)
            sc_ref[rows, :] = jnp.where(pick > 0, 0.0, NEG)
        else:
            sc_ref[rows, :] = jnp.where(blk >= thr_row, 0.0, NEG)

    def bias_loop(general):
        def body(kt, carry):
            bias_tile(kt, general, False)
            return carry
        lax.fori_loop(0, t, body, 0)

    pl.when(has_ties)(lambda: bias_loop(True))
    pl.when(jnp.logical_not(has_ties))(lambda: bias_loop(False))
    bias_tile(t, True, True)

    acc_ref[...] = jnp.zeros_like(acc_ref)
    m_ref[...] = jnp.full_like(m_ref, NEG)

    def logits_stage(kt, buf):
        rows = tile(kt)
        bias = sc_ref[rows, :]
        ka = ka_ref[rows, :]
        tops = []
        for sl in heads:
            x = _dot(ka, qat_ref[:, sl]) + bias
            lg_ref[buf, :, sl] = x
            tops.append(jnp.max(x, axis=0, keepdims=True))
        return jnp.concatenate(tops, axis=0)

    def softmax_stage(kt, buf, top):
        vt = vt_ref[:, tile(kt)]
        m_old = m_ref[...]
        m_new = jnp.maximum(m_old, top)
        m_ref[...] = m_new
        alpha = jnp.exp2(m_old - m_new)
        for h, sl in enumerate(heads):
            p = jnp.exp2(lg_ref[buf, :, sl] - m_new[h:h + 1, :]).astype(_BF16)
            acc_ref[:, sl] = acc_ref[:, sl] * alpha[h:h + 1, :] + _dot(vt, p)

    def attn_body(i, top0):
        top1 = logits_stage(2 * i + 1, 1)
        softmax_stage(2 * i, 0, top0)
        top0 = logits_stage(jnp.minimum(2 * i + 2, t), 0)
        softmax_stage(2 * i + 1, 1, top1)
        return top0

    top_last = lax.fori_loop(0, nkt // 2, attn_body, logits_stage(0, 0))
    pl.when(nkt % 2 == 1)(lambda: softmax_stage(t, 0, top_last))

    for h, sl in enumerate(heads):
        o = acc_ref[:DH_A, sl] / acc_ref[DH_A:DH_A + 1, sl]
        out_ref[:, h * DH_A:(h + 1) * DH_A] = o.T.astype(_BF16)


def _dsa(qat, qit, wit, ka, ki, vt, B, S):
    nt = S // TQ
    n_sel = min(TOPK_MAX, S // 4)
    idx_bits = max(1, int(np.ceil(np.log2(S))))
    kern = functools.partial(_dsa_kernel, n_sel=n_sel, idx_bits=idx_bits)
    return pl.pallas_call(
        kern,
        out_shape=jax.ShapeDtypeStruct((B * S, W_A), _BF16),
        grid=(B, nt),
        in_specs=[
            pl.BlockSpec((None, DH_IDX, HQ), lambda b, t: (b, 0, t)),
            pl.BlockSpec((None, 1, HQ), lambda b, t: (b, 0, t)),
            pl.BlockSpec((None, DH_A, HQ), lambda b, t: (b, 0, t)),
            pl.BlockSpec((None, S, DH_IDX), lambda b, t: (b, 0, 0)),
            pl.BlockSpec((None, S, DH_A), lambda b, t: (b, 0, 0)),
            pl.BlockSpec((None, VT_ROWS, S), lambda b, t: (b, 0, 0)),
        ],
        out_specs=pl.BlockSpec((TQ, W_A), lambda b, t: (b * nt + t, 0)),
        scratch_shapes=[
            pltpu.VMEM((S, TQ), _F32),
            pltpu.VMEM((S, TQ), jnp.int16),
            pltpu.VMEM((S, TQ), jnp.int16),
            pltpu.VMEM((2, TK, HQ), _F32),
            pltpu.VMEM((VT_ROWS, HQ), _F32),
            pltpu.VMEM((N_HEADS_A, TQ), _F32),
            pltpu.VMEM((SUBLANES, TQ), jnp.int32),
        ],
        compiler_params=_params("parallel", "arbitrary"),
        name="dsa",
    )(qit, wit, qat, ki, ka, vt)


def _merge_kernel(hm_ref, ha_ref, gm_ref, ga_ref, x_ref, wbm_ref, wba_ref, wo_ref, g_ref, out_ref):
    ym = _dot(hm_ref[...], wbm_ref[...])
    ya = _dot(ha_ref[...], wba_ref[...])
    y = jax.nn.sigmoid(gm_ref[...].astype(_F32)) * ym + jax.nn.sigmoid(ga_ref[...].astype(_F32)) * ya
    z = _dot(y.astype(_BF16), wo_ref[...])
    zn = z * lax.rsqrt(jnp.mean(z * z, axis=-1, keepdims=True) + EPS) * g_ref[...]
    out_ref[...] = x_ref[...] + zn


def _merge(hm, ha, p16, x2, wbm, wba, wo, g):
    n = x2.shape[0]
    tm = 512
    rows = lambda c: pl.BlockSpec((tm, D_MODEL), lambda i: (i, c))
    return pl.pallas_call(
        _merge_kernel,
        out_shape=jax.ShapeDtypeStruct((n, D_MODEL), _F32),
        grid=(n // tm,),
        in_specs=[rows(0), rows(0), rows(OFF_GM // D_MODEL), rows(OFF_GA // D_MODEL), rows(0),
                  _resident((W_M, D_MODEL)), _resident((W_A, D_MODEL)), _resident((D_MODEL, D_MODEL)),
                  _resident((1, D_MODEL))],
        out_specs=rows(0),
        compiler_params=_params("parallel"),
        name="merge",
    )(hm, ha, p16, p16, x2, wbm, wba, wo, g)


def _ffn_kernel(x_ref, gpre_ref, wg_ref, wu_ref, wd_ref, gpost_ref, out_ref):
    x = x_ref[...]
    f = (x * lax.rsqrt(jnp.mean(x * x, axis=-1, keepdims=True) + EPS) * gpre_ref[...]).astype(_BF16)
    a = _dot(f, wg_ref[...])
    u = _dot(f, wu_ref[...])
    act = (a * jax.nn.sigmoid(a) * u).astype(_BF16)
    z = _dot(act, wd_ref[...])
    out_ref[...] = x + z * lax.rsqrt(jnp.mean(z * z, axis=-1, keepdims=True) + EPS) * gpost_ref[...]


def _ffn(x2, gpre, wg, wu, wd, gpost):
    n = x2.shape[0]
    tm = 512
    rows = pl.BlockSpec((tm, D_MODEL), lambda i: (i, 0))
    return pl.pallas_call(
        _ffn_kernel,
        out_shape=jax.ShapeDtypeStruct((n, D_MODEL), _F32),
        grid=(n // tm,),
        in_specs=[rows, _resident((1, D_MODEL)), _resident((D_MODEL, D_FF)), _resident((D_MODEL, D_FF)),
                  _resident((D_FF, D_MODEL)), _resident((1, D_MODEL))],
        out_specs=rows,
        compiler_params=_params("parallel"),
        name="ffn",
    )(x2, gpre, wg, wu, wd, gpost)


def _pack_in_proj(w, b):
    o = np.cumsum((0, W_M, W_M, W_M, W_M, N_HEADS_M, N_HEADS_M, W_A, DH_A, DH_A,
                   W_IDX, N_HEADS_IDX, DH_IDX, D_MODEL, D_MODEL))
    mq, mk, mv, mo, mi, mf, aq, ak, av, iq, iw, ik, gm, ga = [slice(o[i], o[i + 1]) for i in range(14)]
    order16 = (mq, mk, mv, mo, aq, gm, ga, iq, ak, av)
    order32 = (ik, iw, mi, mf)

    def cat(arr, order, width):
        parts = [arr[..., s] for s in order]
        used = sum(p.shape[-1] for p in parts)
        parts.append(jnp.zeros(arr.shape[:-1] + (width - used,), arr.dtype))
        return jnp.concatenate(parts, axis=-1)

    return (cat(w, order16, N16).astype(_BF16), cat(b, order16, N16)[..., None, :],
            cat(w, order32, N32).astype(_BF16), cat(b, order32, N32)[..., None, :])


def _rope_tables(S):
    def tab(dim):
        inv = ROPE_THETA ** (-jnp.arange(dim // 2, dtype=_F32) / (dim // 2))
        ang = jnp.arange(S, dtype=_F32)[:, None] * inv[None, :]
        c, s = jnp.cos(ang), jnp.sin(ang)
        return jnp.concatenate([c, c], axis=-1), jnp.concatenate([-s, s], axis=-1)

    ca, sa = tab(DH_A)
    ci, si = tab(DH_IDX)
    return (ca.T, sa.T, ci.T, si.T, ca, sa, ci, si)


def kernel(x, norm_mix_pre, norm_mix_post, norm_ffn_pre, norm_ffn_post, w_in, b_in, conv_qk,
           norm_mlstm_head, norm_idx_k, w_branch_mlstm, w_branch_attn, w_out,
           w_ffn_gate, w_ffn_up, w_ffn_down):
    B, S, _ = x.shape
    assert S % ML == 0 and S % TKB == 0 and S % TP == 0 and TQ == TK and (B * S) % 1024 == 0
    depth = w_in.shape[0]
    tabs = _rope_tables(S)
    x2 = x.reshape(B * S, D_MODEL)
    w16, b16, w32, b32 = _pack_in_proj(w_in, b_in)
    wbm, wba, wo = (w.astype(_BF16) for w in (w_branch_mlstm, w_branch_attn, w_out))
    wg, wu, wd = (w.astype(_BF16) for w in (w_ffn_gate, w_ffn_up, w_ffn_down))
    for l in range(depth):
        p16, p32 = _in_proj(x2, norm_mix_pre[l][None, :], w16[l], b16[l], w32[l], b32[l])
        hm = _mlstm(p16, p32, conv_qk[l], norm_mlstm_head[l][None, :], B, S)
        qat, qit, wit, ka, ki, vt = _dsa_prep(p16, p32, norm_idx_k[l][None, :], tabs, B, S)
        ha = _dsa(qat, qit, wit, ka, ki, vt, B, S)
        x2 = _merge(hm, ha, p16, x2, wbm[l], wba[l], wo[l], norm_mix_post[l][None, :])
        x2 = _ffn(x2, norm_ffn_pre[l][None, :], wg[l], wu[l], wd[l], norm_ffn_post[l][None, :])
    return x2.reshape(B, S, D_MODEL)
```

```python
import functools

import jax
import jax.numpy as jnp
import numpy as np
from jax import lax
from jax.experimental import pallas as pl
from jax.experimental.pallas import tpu as pltpu

D_MODEL = 1024
CHUNK = 64
N_HEADS_M = 4
DH_M = 256
W_M = N_HEADS_M * DH_M
CONV_W = 4
N_HEADS_A = 8
DH_A = 128
W_A = N_HEADS_A * DH_A
N_HEADS_IDX = 8
DH_IDX = 64
W_IDX = N_HEADS_IDX * DH_IDX
IDX_W_SCALE = (N_HEADS_IDX ** -0.5) * (DH_IDX ** -0.5)
TOPK_MAX = 256
D_FF = 2816
ROPE_THETA = 10000.0
EPS = 1e-6
NEG = -1e30

LANES = 128
SUBLANES = 8
VMEM_LIMIT_BYTES = 56 * 1024 * 1024

OFF_MQ, OFF_MK, OFF_MV, OFF_MO = 0, 1024, 2048, 3072
OFF_AQ, OFF_GM, OFF_GA = 4096, 5120, 6144
OFF_IQ, OFF_AK, OFF_AV = 7168, 7680, 7808
N16 = 8192
C_IK, C_IW, C_MI, C_MF = 0, 64, 72, 76
N32 = 128

ML = 512
TQ = 256
TK = 256
TKB = 2 * TK
TP = 4 * TQ
HQ = N_HEADS_A * TQ
VT_ROWS = DH_A + SUBLANES
INT_MIN = -2 ** 31
LOG2E = 1.4426950408889634
QK_SCALE2 = (DH_A ** -0.5) * LOG2E

_F32 = jnp.float32
_BF16 = jnp.bfloat16


def _dot(a, b):
    return jnp.dot(a, b, preferred_element_type=_F32)


def _dot_nt(a, b):
    return lax.dot_general(a, b, (((1,), (1,)), ((), ())), preferred_element_type=_F32)


def _dot_tn(a, b):
    return lax.dot_general(a, b, (((0,), (0,)), ((), ())), preferred_element_type=_F32)


def _params(*sem):
    return pltpu.CompilerParams(dimension_semantics=sem, vmem_limit_bytes=VMEM_LIMIT_BYTES)


def _resident(shape, layer=None):
    n = len(shape)
    if layer is None:
        return pl.BlockSpec(shape, lambda *_: (0,) * n, pipeline_mode=pl.Buffered(1))
    return pl.BlockSpec((None,) + shape, lambda *_: (layer,) + (0,) * n, pipeline_mode=pl.Buffered(1))


def _in_proj_kernel(x_ref, g_ref, w_ref, b_ref, ws_ref, bs_ref, o16_ref, o32_ref, h_ref):
    @pl.when(pl.program_id(1) == 0)
    def _():
        x = x_ref[...]
        y = x * lax.rsqrt(jnp.mean(x * x, axis=-1, keepdims=True) + EPS)
        hb = (y * g_ref[...]).astype(_BF16)
        h_ref[...] = hb
        o32_ref[...] = _dot(hb, ws_ref[...]) + bs_ref[...]

    o16_ref[...] = (_dot(h_ref[...], w_ref[...]) + b_ref[...]).astype(_BF16)


def _in_proj(x2, g, w16, b16, w32, b32, l):
    n = x2.shape[0]
    tm, tn = 1024, 2048
    return pl.pallas_call(
        _in_proj_kernel,
        out_shape=(jax.ShapeDtypeStruct((n, N16), _BF16), jax.ShapeDtypeStruct((n, N32), _F32)),
        grid=(n // tm, N16 // tn),
        in_specs=[
            pl.BlockSpec((tm, D_MODEL), lambda i, j: (i, 0)),
            pl.BlockSpec((1, D_MODEL), lambda i, j: (0, 0)),
            pl.BlockSpec((None, D_MODEL, tn), lambda i, j: (l, 0, j)),
            pl.BlockSpec((None, 1, tn), lambda i, j: (l, 0, j)),
            pl.BlockSpec((None, D_MODEL, N32), lambda i, j: (l, 0, 0)),
            pl.BlockSpec((None, 1, N32), lambda i, j: (l, 0, 0)),
        ],
        out_specs=(
            pl.BlockSpec((tm, tn), lambda i, j: (i, j)),
            pl.BlockSpec((tm, N32), lambda i, j: (i, 0)),
        ),
        scratch_shapes=[pltpu.VMEM((tm, D_MODEL), _BF16)],
        compiler_params=_params("parallel", "arbitrary"),
        name="in_proj",
    )(x2, g, w16, b16, w32, b32)


def _log_sigmoid(x):
    return jnp.minimum(x, 0.0) - jnp.log(1.0 + jnp.exp(-jnp.abs(x)))


def _cumsum_rows(x):
    n = x.shape[0]
    row = lax.broadcasted_iota(jnp.int32, x.shape, 0)
    k = 1
    while k < n:
        x = x + jnp.where(row >= k, pltpu.roll(x, k, axis=0), 0.0)
        k *= 2
    return x


def _mlstm_kernel(qk_ref, v_ref, o_ref, g32_ref, cw_ref, hn_ref, out_ref,
                  ct_ref, n_ref, m_ref, tail_ref):
    @pl.when(pl.program_id(1) == 0)
    def _():
        ct_ref[...] = jnp.zeros_like(ct_ref)
        n_ref[...] = jnp.zeros_like(n_ref)
        m_ref[...] = jnp.zeros_like(m_ref)
        tail_ref[...] = jnp.zeros_like(tail_ref)

    L = ML
    qk_raw = qk_ref[...].astype(_F32)
    ext = jnp.concatenate([tail_ref[...], qk_raw], axis=0)
    tail_ref[...] = qk_raw[L - SUBLANES:, :]
    cw = cw_ref[...]
    conv = ext[SUBLANES:, :] * cw[CONV_W - 1:CONV_W, :]
    for j in range(1, CONV_W):
        conv = conv + pltpu.roll(ext, j, axis=0)[SUBLANES:, :] * cw[CONV_W - 1 - j:CONV_W - j, :]
    half = 0.5 * conv
    qk = half + half * jnp.tanh(half)

    g32 = g32_ref[...]
    bc = _cumsum_rows(_log_sigmoid(g32))
    g32_t = g32.T
    bc_t = bc.T
    row = lax.broadcasted_iota(jnp.int32, (L, L), 0)
    col = lax.broadcasted_iota(jnp.int32, (L, L), 1)
    causal = col <= row

    for h in range(N_HEADS_M):
        sl = slice(h * DH_M, (h + 1) * DH_M)
        q = qk[:, sl].astype(_BF16)
        k = (qk[:, W_M + h * DH_M:W_M + (h + 1) * DH_M] * (DH_M ** -0.5)).astype(_BF16)
        v = v_ref[:, sl]
        b_c = bc[:, C_MF + h:C_MF + h + 1]
        i_c = g32[:, C_MI + h:C_MI + h + 1]
        b_r = bc_t[C_MF + h:C_MF + h + 1, :]
        i_r = g32_t[C_MI + h:C_MI + h + 1, :]
        m_prev = m_ref[h]

        dlog = jnp.where(causal, b_c - b_r + i_r, NEG)
        inter = b_c + m_prev
        mj = jnp.maximum(inter, jnp.max(dlog, axis=-1, keepdims=True))
        dw = jnp.exp(dlog - mj)
        iw = jnp.exp(inter - mj)
        s = _dot_nt(q, k) * dw
        num = iw * _dot(q, ct_ref[h].astype(_BF16)) + _dot(s.astype(_BF16), v)
        qn = jnp.sum(q.astype(_F32) * n_ref[h], axis=-1, keepdims=True)
        den = iw * qn + jnp.sum(s, axis=-1, keepdims=True)
        hh = num / jnp.maximum(jnp.abs(den), jnp.exp(-mj))

        b_last = b_c[L - 1:L, :]
        gg = b_last - b_c + i_c
        m_new = jnp.maximum(b_last + m_prev, jnp.max(gg, axis=0, keepdims=True))
        decay = jnp.exp(b_last + m_prev - m_new)
        w = jnp.exp(gg - m_new)
        kf = k.astype(_F32)
        wv = (w * v.astype(_F32)).astype(_BF16)
        ct_ref[h] = decay * ct_ref[h] + _dot_tn(k, wv)
        n_ref[h] = decay * n_ref[h] + jnp.sum(w * kf, axis=0, keepdims=True)
        m_ref[h] = m_new

        y = hh * lax.rsqrt(jnp.mean(hh * hh, axis=-1, keepdims=True) + EPS) * hn_ref[:, sl]
        gate = jax.nn.sigmoid(o_ref[:, sl].astype(_F32))
        out_ref[:, sl] = (y * gate).astype(_BF16)


def _mlstm(p16, p32, conv_qk, norm_head, B, S):
    n = B * S
    nc = S // ML
    row = lambda b, c: b * nc + c
    return pl.pallas_call(
        _mlstm_kernel,
        out_shape=jax.ShapeDtypeStruct((n, W_M), _BF16),
        grid=(B, nc),
        in_specs=[
            pl.BlockSpec((ML, 2 * W_M), lambda b, c: (row(b, c), 0)),
            pl.BlockSpec((ML, W_M), lambda b, c: (row(b, c), OFF_MV // W_M)),
            pl.BlockSpec((ML, W_M), lambda b, c: (row(b, c), OFF_MO // W_M)),
            pl.BlockSpec((ML, N32), lambda b, c: (row(b, c), 0)),
            pl.BlockSpec((CONV_W, 2 * W_M), lambda b, c: (0, 0)),
            pl.BlockSpec((1, W_M), lambda b, c: (0, 0)),
        ],
        out_specs=pl.BlockSpec((ML, W_M), lambda b, c: (row(b, c), 0)),
        scratch_shapes=[
            pltpu.VMEM((N_HEADS_M, DH_M, DH_M), _F32),
            pltpu.VMEM((N_HEADS_M, 1, DH_M), _F32),
            pltpu.VMEM((N_HEADS_M, 1, 1), _F32),
            pltpu.VMEM((SUBLANES, 2 * W_M), _F32),
        ],
        compiler_params=_params("parallel", "arbitrary"),
        name="mlstm",
    )(p16, p16, p16, p32, conv_qk, norm_head)


def _rot_rows(x, half):
    return jnp.concatenate([x[half:], x[:half]], axis=0)


def _dsa_prep_kernel(aq_ref, iq_ref, ak_ref, av_ref, g32_ref, gk_ref,
                     cat_ref, sat_ref, cit_ref, sit_ref, ca_ref, sa_ref, ci_ref, si_ref,
                     qat_ref, qit_ref, wit_ref, ka_ref, ki_ref, vt_ref):
    for s in range(TP // TQ):
        rows = slice(s * TQ, (s + 1) * TQ)
        slab = s * HQ
        cat, sat = cat_ref[:, rows], sat_ref[:, rows]
        aq_t = aq_ref[rows, :].astype(_F32).T
        for h in range(N_HEADS_A):
            x = aq_t[h * DH_A:(h + 1) * DH_A]
            qat_ref[:, slab + h * TQ:slab + (h + 1) * TQ] = (
                (x * cat + _rot_rows(x, DH_A // 2) * sat) * QK_SCALE2).astype(_BF16)

        cit, sit = cit_ref[:, rows], sit_ref[:, rows]
        iq_t = iq_ref[rows, :].astype(_F32).T
        for h in range(N_HEADS_IDX):
            x = iq_t[h * DH_IDX:(h + 1) * DH_IDX]
            qit_ref[:, slab + h * TQ:slab + (h + 1) * TQ] = (
                x * cit + _rot_rows(x, DH_IDX // 2) * sit).astype(_BF16)

        w_t = g32_ref[rows, :].T[C_IW:C_IW + N_HEADS_IDX, :] * IDX_W_SCALE
        for h in range(N_HEADS_IDX):
            wit_ref[:, slab + h * TQ:slab + (h + 1) * TQ] = w_t[h:h + 1, :]

    ak = ak_ref[...].astype(_F32)
    ka_ref[...] = (ak * ca_ref[...] + pltpu.roll(ak, DH_A // 2, axis=1) * sa_ref[...]).astype(_BF16)

    ik = g32_ref[:, C_IK:C_IK + DH_IDX]
    xc = ik - jnp.mean(ik, axis=-1, keepdims=True)
    ln = xc * lax.rsqrt(jnp.mean(xc * xc, axis=-1, keepdims=True) + EPS) * gk_ref[...]
    half = DH_IDX // 2
    ln_rot = jnp.concatenate([ln[:, half:], ln[:, :half]], axis=1)
    ki_ref[...] = (ln * ci_ref[...] + ln_rot * si_ref[...]).astype(_BF16)

    vt_ref[:DH_A, :] = av_ref[...].astype(_F32).T.astype(_BF16)
    vt_ref[DH_A:, :] = jnp.ones((VT_ROWS - DH_A, TP), _BF16)


def _dsa_prep(p16, p32, norm_idx_k, tabs, B, S):
    nt = S // TQ
    npt = S // TP
    row = lambda b, t: b * npt + t
    cat, sat, cit, sit, ca, sa, ci, si = tabs
    return pl.pallas_call(
        _dsa_prep_kernel,
        out_shape=(
            jax.ShapeDtypeStruct((B, DH_A, nt * HQ), _BF16),
            jax.ShapeDtypeStruct((B, DH_IDX, nt * HQ), _BF16),
            jax.ShapeDtypeStruct((B, 1, nt * HQ), _F32),
            jax.ShapeDtypeStruct((B, S, DH_A), _BF16),
            jax.ShapeDtypeStruct((B, S, DH_IDX), _BF16),
            jax.ShapeDtypeStruct((B, VT_ROWS, S), _BF16),
        ),
        grid=(B, npt),
        in_specs=[
            pl.BlockSpec((TP, W_A), lambda b, t: (row(b, t), OFF_AQ // W_A)),
            pl.BlockSpec((TP, W_IDX), lambda b, t: (row(b, t), OFF_IQ // W_IDX)),
            pl.BlockSpec((TP, DH_A), lambda b, t: (row(b, t), OFF_AK // DH_A)),
            pl.BlockSpec((TP, DH_A), lambda b, t: (row(b, t), OFF_AV // DH_A)),
            pl.BlockSpec((TP, N32), lambda b, t: (row(b, t), 0)),
            pl.BlockSpec((1, DH_IDX), lambda b, t: (0, 0)),
            pl.BlockSpec((DH_A, TP), lambda b, t: (0, t)),
            pl.BlockSpec((DH_A, TP), lambda b, t: (0, t)),
            pl.BlockSpec((DH_IDX, TP), lambda b, t: (0, t)),
            pl.BlockSpec((DH_IDX, TP), lambda b, t: (0, t)),
            pl.BlockSpec((TP, DH_A), lambda b, t: (t, 0)),
            pl.BlockSpec((TP, DH_A), lambda b, t: (t, 0)),
            pl.BlockSpec((TP, DH_IDX), lambda b, t: (t, 0)),
            pl.BlockSpec((TP, DH_IDX), lambda b, t: (t, 0)),
        ],
        out_specs=(
            pl.BlockSpec((None, DH_A, (TP // TQ) * HQ), lambda b, t: (b, 0, t)),
            pl.BlockSpec((None, DH_IDX, (TP // TQ) * HQ), lambda b, t: (b, 0, t)),
            pl.BlockSpec((None, 1, (TP // TQ) * HQ), lambda b, t: (b, 0, t)),
            pl.BlockSpec((None, TP, DH_A), lambda b, t: (b, t, 0)),
            pl.BlockSpec((None, TP, DH_IDX), lambda b, t: (b, t, 0)),
            pl.BlockSpec((None, VT_ROWS, TP), lambda b, t: (b, 0, t)),
        ),
        compiler_params=_params("parallel", "parallel"),
        name="dsa_prep",
    )(p16, p16, p16, p16, p32, norm_idx_k, cat, sat, cit, sit, ca, sa, ci, si)


def _allsum_sublanes(x):
    for k in (4, 2, 1):
        x = x + pltpu.roll(x, k, axis=0)
    return x


def _key_to_f32(key):
    return pltpu.bitcast(key ^ ((key >> 31) & 0x7FFFFFFF), _F32)


def _dsa_kernel(qit_ref, wit_ref, qat_ref, ki_ref, ka_ref, vt_ref, out_ref,
                sc_ref, hi_ref, lo_ref, lg_ref, acc_ref, m_ref, jmax_ref,
                *, n_sel, idx_bits):
    t = pl.program_id(1)
    nkt = t + 1
    n_keys = nkt * TK
    row_in_tile = lax.broadcasted_iota(jnp.int32, (TK, TQ), 0)
    lane_in_tile = lax.broadcasted_iota(jnp.int32, (TK, TQ), 1)
    adm_diag = (row_in_tile // CHUNK) <= (lane_in_tile // CHUNK)
    heads = [slice(h * TQ, (h + 1) * TQ) for h in range(N_HEADS_A)]
    tile = lambda kt: pl.ds(pl.multiple_of(kt * TK, TK), TK)
    i16 = jnp.int16
    P16 = 2 * SUBLANES

    def score_block(r0, n, diag):
        rows = pl.ds(r0, n)
        r_all = _dot(ki_ref[rows, :], qit_ref[...])
        sc = jnp.zeros((n, TQ), _F32)
        for sl in heads:
            sc = sc + wit_ref[:, sl] * jnp.maximum(r_all[:, sl], 0.0)
        if diag:
            sc = jnp.where(adm_diag, sc, NEG)
        sc = jnp.where(sc == 0.0, 0.0, sc)
        sc_ref[rows, :] = sc
        bits = pltpu.bitcast(sc, jnp.int32)
        key = bits ^ ((bits >> 31) & 0x7FFFFFFF)
        hi_ref[rows, :] = (key >> 16).astype(i16)
        lo_ref[rows, :] = ((key & 0xFFFF) - 32768).astype(i16)

    def score_body(i, carry):
        score_block(pl.multiple_of(i * TKB, TKB), TKB, False)
        return carry

    lax.fori_loop(0, t // 2, score_body, 0)
    pl.when(t % 2 == 1)(lambda: score_block(pl.multiple_of((t - 1) * TK, TK), TK, False))
    score_block(pl.multiple_of(t * TK, TK), TK, True)

    nkb = (nkt + 1) // 2
    block = lambda i: pl.ds(pl.multiple_of(i * TKB, TKB), TKB)

    @pl.when(nkt % 2 == 1)
    def _():
        pad = jnp.full((TK, TQ), -32768, i16)
        hi_ref[tile(nkt), :] = pad
        lo_ref[tile(nkt), :] = pad

    def count16(ref, pred):
        def body(i, accs):
            blk = ref[block(i), :]
            accs = list(accs)
            for g in range(TKB // P16):
                accs[g % 4] = accs[g % 4] + pred(blk[g * P16:(g + 1) * P16])
            return tuple(accs)
        z = jnp.zeros((P16, TQ), i16)
        a = [x.astype(jnp.int32) for x in lax.fori_loop(0, nkb, body, (z, z, z, z))]
        s = (a[0] + a[1]) + (a[2] + a[3])
        return _allsum_sublanes(s[:SUBLANES] + s[SUBLANES:])

    def rows16(x):
        return jnp.concatenate([x, x], axis=0).astype(i16)

    def count_ge16(ref, cand):
        c16 = rows16(cand)
        return count16(ref, lambda b: jnp.where(b >= c16, i16(1), i16(0)))

    def search16(ref, need, c_base):
        def step(i, carry):
            cur, cnt = carry
            cand = cur + lax.shift_left(jnp.int32(1), 15 - i)
            c = count_ge16(ref, cand)
            ok = c >= need
            return jnp.where(ok, cand, cur), jnp.where(ok, c, cnt)
        return lax.fori_loop(0, 16, step, (jnp.full((SUBLANES, TQ), -32768, jnp.int32), c_base))

    zeros8 = jnp.zeros((SUBLANES, TQ), jnp.int32)
    t_hi, c_hi = search16(hi_ref, zeros8 + n_sel, zeros8 + n_keys)
    c_gt = jnp.where(t_hi >= 32767, 0, count_ge16(hi_ref, jnp.minimum(t_hi + 1, 32767)))
    t_hi16 = rows16(t_hi)

    def narrow_body(i, carry):
        rows = block(i)
        hi, lo = hi_ref[rows, :], lo_ref[rows, :]
        lo_ref[rows, :] = jnp.concatenate(
            [jnp.where(hi[g * P16:(g + 1) * P16] == t_hi16, lo[g * P16:(g + 1) * P16], i16(-32768))
             for g in range(TKB // P16)], axis=0)
        return carry

    lax.fori_loop(0, nkb, narrow_body, 0)
    t_lo, c_lo = search16(lo_ref, n_sel - c_gt, c_hi - c_gt)
    cge = c_gt + c_lo
    thr = _key_to_f32((t_hi << 16) | ((t_lo + 32768) & 0xFFFF))
    thr_row = thr[0:1, :]

    def count32(pred):
        def body(kt, accs):
            r0 = pl.multiple_of(kt * TK, TK)
            blk = sc_ref[pl.ds(r0, TK), :]
            accs = list(accs)
            for g in range(TK // SUBLANES):
                accs[g % 4] = accs[g % 4] + pred(blk[g * SUBLANES:(g + 1) * SUBLANES], r0 + g * SUBLANES)
            return tuple(accs)
        a = lax.fori_loop(0, nkt, body, (zeros8, zeros8, zeros8, zeros8))
        return _allsum_sublanes((a[0] + a[1]) + (a[2] + a[3]))

    sub = lax.broadcasted_iota(jnp.int32, (SUBLANES, TQ), 0)
    jmax_ref[...] = jnp.full_like(jmax_ref, 2 ** 30)
    has_ties = jnp.max(cge) > n_sel

    @pl.when(has_ties)
    def _():
        need = n_sel - count32(lambda blk, r: jnp.where(blk > thr, 1, 0))

        def step(i, j):
            cand = j + lax.shift_left(jnp.int32(1), idx_bits - 1 - i)
            c = count32(lambda blk, r: jnp.where(blk == thr, jnp.where(r + sub < cand, 1, 0), 0))
            return jnp.where(c < need, cand, j)

        jmax_ref[...] = lax.fori_loop(0, idx_bits, step, zeros8)

    def bias_tile(kt, general, diag):
        rows = tile(kt)
        blk = sc_ref[rows, :]
        if general:
            krow = kt * TK + row_in_tile
            pick = jnp.where(blk > thr_row, 1,
                             jnp.where(blk == thr_row, jnp.where(krow <= jmax_ref[0:1, :], 1, 0), 0))
            if diag:
                pick = jnp.where(adm_diag, pick, 0)
            sc_ref[rows, :] = jnp.where(pick > 0, 0.0, NEG)
        else:
            sc_ref[rows, :] = jnp.where(blk >= thr_row, 0.0, NEG)

    def bias_loop(general):
        def body(kt, carry):
            bias_tile(kt, general, False)
            return carry
        lax.fori_loop(0, t, body, 0)

    pl.when(has_ties)(lambda: bias_loop(True))
    pl.when(jnp.logical_not(has_ties))(lambda: bias_loop(False))
    bias_tile(t, True, True)

    acc_ref[...] = jnp.zeros_like(acc_ref)
    m_ref[...] = jnp.full_like(m_ref, NEG)

    def logits_stage(kt, buf):
        rows = tile(kt)
        bias = sc_ref[rows, :]
        ka = ka_ref[rows, :]
        tops = []
        for sl in heads:
            x = _dot(ka, qat_ref[:, sl]) + bias
            lg_ref[buf, :, sl] = x
            tops.append(jnp.max(x, axis=0, keepdims=True))
        return jnp.concatenate(tops, axis=0)

    def softmax_stage(kt, buf, top):
        vt = vt_ref[:, tile(kt)]
        m_old = m_ref[...]
        m_new = jnp.maximum(m_old, top)
        m_ref[...] = m_new
        alpha = jnp.exp2(m_old - m_new)
        for h, sl in enumerate(heads):
            p = jnp.exp2(lg_ref[buf, :, sl] - m_new[h:h + 1, :]).astype(_BF16)
            acc_ref[:, sl] = acc_ref[:, sl] * alpha[h:h + 1, :] + _dot(vt, p)

    def attn_body(i, top0):
        top1 = logits_stage(2 * i + 1, 1)
        softmax_stage(2 * i, 0, top0)
        top0 = logits_stage(jnp.minimum(2 * i + 2, t), 0)
        softmax_stage(2 * i + 1, 1, top1)
        return top0

    top_last = lax.fori_loop(0, nkt // 2, attn_body, logits_stage(0, 0))
    pl.when(nkt % 2 == 1)(lambda: softmax_stage(t, 0, top_last))

    for h, sl in enumerate(heads):
        o = acc_ref[:DH_A, sl] / acc_ref[DH_A:DH_A + 1, sl]
        out_ref[:, h * DH_A:(h + 1) * DH_A] = o.T.astype(_BF16)


def _dsa(qat, qit, wit, ka, ki, vt, B, S):
    nt = S // TQ
    n_sel = min(TOPK_MAX, S // 4)
    idx_bits = max(1, int(np.ceil(np.log2(S))))
    kern = functools.partial(_dsa_kernel, n_sel=n_sel, idx_bits=idx_bits)
    return pl.pallas_call(
        kern,
        out_shape=jax.ShapeDtypeStruct((B * S, W_A), _BF16),
        grid=(B, nt),
        in_specs=[
            pl.BlockSpec((None, DH_IDX, HQ), lambda b, t: (b, 0, t)),
            pl.BlockSpec((None, 1, HQ), lambda b, t: (b, 0, t)),
            pl.BlockSpec((None, DH_A, HQ), lambda b, t: (b, 0, t)),
            pl.BlockSpec((None, S, DH_IDX), lambda b, t: (b, 0, 0)),
            pl.BlockSpec((None, S, DH_A), lambda b, t: (b, 0, 0)),
            pl.BlockSpec((None, VT_ROWS, S), lambda b, t: (b, 0, 0)),
        ],
        out_specs=pl.BlockSpec((TQ, W_A), lambda b, t: (b * nt + t, 0)),
        scratch_shapes=[
            pltpu.VMEM((S, TQ), _F32),
            pltpu.VMEM((S, TQ), jnp.int16),
            pltpu.VMEM((S, TQ), jnp.int16),
            pltpu.VMEM((2, TK, HQ), _F32),
            pltpu.VMEM((VT_ROWS, HQ), _F32),
            pltpu.VMEM((N_HEADS_A, TQ), _F32),
            pltpu.VMEM((SUBLANES, TQ), jnp.int32),
        ],
        compiler_params=_params("parallel", "arbitrary"),
        name="dsa",
    )(qit, wit, qat, ki, ka, vt)


def _merge_kernel(hm_ref, ha_ref, gm_ref, ga_ref, x_ref, wbm_ref, wba_ref, wo_ref, g_ref, out_ref):
    ym = _dot(hm_ref[...], wbm_ref[...])
    ya = _dot(ha_ref[...], wba_ref[...])
    y = jax.nn.sigmoid(gm_ref[...].astype(_F32)) * ym + jax.nn.sigmoid(ga_ref[...].astype(_F32)) * ya
    z = _dot(y.astype(_BF16), wo_ref[...])
    zn = z * lax.rsqrt(jnp.mean(z * z, axis=-1, keepdims=True) + EPS) * g_ref[...]
    out_ref[...] = x_ref[...] + zn


def _merge(hm, ha, p16, x2, wbm, wba, wo, g, l):
    n = x2.shape[0]
    tm = 512
    rows = lambda c: pl.BlockSpec((tm, D_MODEL), lambda i: (i, c))
    return pl.pallas_call(
        _merge_kernel,
        out_shape=jax.ShapeDtypeStruct((n, D_MODEL), _F32),
        grid=(n // tm,),
        in_specs=[rows(0), rows(0), rows(OFF_GM // D_MODEL), rows(OFF_GA // D_MODEL), rows(0),
                  _resident((W_M, D_MODEL), l), _resident((W_A, D_MODEL), l), _resident((D_MODEL, D_MODEL), l),
                  _resident((1, D_MODEL))],
        out_specs=rows(0),
        compiler_params=_params("parallel"),
        name="merge",
    )(hm, ha, p16, p16, x2, wbm, wba, wo, g)


def _ffn_kernel(x_ref, gpre_ref, wg_ref, wu_ref, wd_ref, gpost_ref, out_ref):
    x = x_ref[...]
    f = (x * lax.rsqrt(jnp.mean(x * x, axis=-1, keepdims=True) + EPS) * gpre_ref[...]).astype(_BF16)
    a = _dot(f, wg_ref[...])
    u = _dot(f, wu_ref[...])
    act = (a * jax.nn.sigmoid(a) * u).astype(_BF16)
    z = _dot(act, wd_ref[...])
    out_ref[...] = x + z * lax.rsqrt(jnp.mean(z * z, axis=-1, keepdims=True) + EPS) * gpost_ref[...]


def _ffn(x2, gpre, wg, wu, wd, gpost, l):
    n = x2.shape[0]
    tm = 512
    rows = pl.BlockSpec((tm, D_MODEL), lambda i: (i, 0))
    return pl.pallas_call(
        _ffn_kernel,
        out_shape=jax.ShapeDtypeStruct((n, D_MODEL), _F32),
        grid=(n // tm,),
        in_specs=[rows, _resident((1, D_MODEL)), _resident((D_MODEL, D_FF), l), _resident((D_MODEL, D_FF), l),
                  _resident((D_FF, D_MODEL), l), _resident((1, D_MODEL))],
        out_specs=rows,
        compiler_params=_params("parallel"),
        name="ffn",
    )(x2, gpre, wg, wu, wd, gpost)


def _pack_in_proj(w, b):
    o = np.cumsum((0, W_M, W_M, W_M, W_M, N_HEADS_M, N_HEADS_M, W_A, DH_A, DH_A,
                   W_IDX, N_HEADS_IDX, DH_IDX, D_MODEL, D_MODEL))
    mq, mk, mv, mo, mi, mf, aq, ak, av, iq, iw, ik, gm, ga = [slice(o[i], o[i + 1]) for i in range(14)]
    order16 = (mq, mk, mv, mo, aq, gm, ga, iq, ak, av)
    order32 = (ik, iw, mi, mf)

    def cat(arr, order, width):
        parts = [arr[..., s] for s in order]
        used = sum(p.shape[-1] for p in parts)
        parts.append(jnp.zeros(arr.shape[:-1] + (width - used,), arr.dtype))
        return jnp.concatenate(parts, axis=-1)

    return (cat(w, order16, N16).astype(_BF16), cat(b, order16, N16)[..., None, :],
            cat(w, order32, N32).astype(_BF16), cat(b, order32, N32)[..., None, :])


def _rope_tables(S):
    def tab(dim):
        inv = ROPE_THETA ** (-jnp.arange(dim // 2, dtype=_F32) / (dim // 2))
        ang = jnp.arange(S, dtype=_F32)[:, None] * inv[None, :]
        c, s = jnp.cos(ang), jnp.sin(ang)
        return jnp.concatenate([c, c], axis=-1), jnp.concatenate([-s, s], axis=-1)

    ca, sa = tab(DH_A)
    ci, si = tab(DH_IDX)
    return (ca.T, sa.T, ci.T, si.T, ca, sa, ci, si)


def kernel(x, norm_mix_pre, norm_mix_post, norm_ffn_pre, norm_ffn_post, w_in, b_in, conv_qk,
           norm_mlstm_head, norm_idx_k, w_branch_mlstm, w_branch_attn, w_out,
           w_ffn_gate, w_ffn_up, w_ffn_down):
    B, S, _ = x.shape
    assert S % ML == 0 and S % TKB == 0 and S % TP == 0 and TQ == TK and (B * S) % 1024 == 0
    depth = w_in.shape[0]
    tabs = _rope_tables(S)
    x2 = x.reshape(B * S, D_MODEL)
    w16, b16, w32, b32 = _pack_in_proj(w_in, b_in)
    wbm, wba, wo = (w.astype(_BF16) for w in (w_branch_mlstm, w_branch_attn, w_out))
    wg, wu, wd = (w.astype(_BF16) for w in (w_ffn_gate, w_ffn_up, w_ffn_down))
    for l in range(depth):
        p16, p32 = _in_proj(x2, norm_mix_pre[l][None, :], w16, b16, w32, b32, l)
        hm = _mlstm(p16, p32, conv_qk[l], norm_mlstm_head[l][None, :], B, S)
        qat, qit, wit, ka, ki, vt = _dsa_prep(p16, p32, norm_idx_k[l][None, :], tabs, B, S)
        ha = _dsa(qat, qit, wit, ka, ki, vt, B, S)
        x2 = _merge(hm, ha, p16, x2, wbm, wba, wo, norm_mix_post[l][None, :], l)
        x2 = _ffn(x2, norm_ffn_pre[l][None, :], wg, wu, wd, norm_ffn_post[l][None, :], l)
    return x2.reshape(B, S, D_MODEL)
```

```python
import functools

import jax
import jax.numpy as jnp
import numpy as np
from jax import lax
from jax.experimental import pallas as pl
from jax.experimental.pallas import tpu as pltpu

D_MODEL = 1024
CHUNK = 64
N_HEADS_M = 4
DH_M = 256
W_M = N_HEADS_M * DH_M
CONV_W = 4
N_HEADS_A = 8
DH_A = 128
W_A = N_HEADS_A * DH_A
N_HEADS_IDX = 8
DH_IDX = 64
W_IDX = N_HEADS_IDX * DH_IDX
IDX_W_SCALE = (N_HEADS_IDX ** -0.5) * (DH_IDX ** -0.5)
TOPK_MAX = 256
D_FF = 2816
ROPE_THETA = 10000.0
EPS = 1e-6
NEG = -1e30

LANES = 128
SUBLANES = 8
VMEM_LIMIT_BYTES = 56 * 1024 * 1024

OFF_MQ, OFF_MK, OFF_MV, OFF_MO = 0, 1024, 2048, 3072
OFF_AQ, OFF_GM, OFF_GA = 4096, 5120, 6144
OFF_IQ, OFF_AK, OFF_AV = 7168, 7680, 7808
N16 = 8192
C_IK, C_IW, C_MI, C_MF = 0, 64, 72, 76
N32 = 128

ML = 512
TQ = 256
TK = 256
TKB = 2 * TK
TP = 4 * TQ
HQ = N_HEADS_A * TQ
VT_ROWS = DH_A + SUBLANES
INT_MIN = -2 ** 31
LOG2E = 1.4426950408889634
QK_SCALE2 = (DH_A ** -0.5) * LOG2E

_F32 = jnp.float32
_BF16 = jnp.bfloat16


def _dot(a, b):
    return jnp.dot(a, b, preferred_element_type=_F32)


def _dot_nt(a, b):
    return lax.dot_general(a, b, (((1,), (1,)), ((), ())), preferred_element_type=_F32)


def _dot_tn(a, b):
    return lax.dot_general(a, b, (((0,), (0,)), ((), ())), preferred_element_type=_F32)


def _params(*sem):
    return pltpu.CompilerParams(dimension_semantics=sem, vmem_limit_bytes=VMEM_LIMIT_BYTES)


def _resident(shape, layer=None):
    n = len(shape)
    if layer is None:
        return pl.BlockSpec(shape, lambda *_: (0,) * n, pipeline_mode=pl.Buffered(1))
    return pl.BlockSpec((None,) + shape, lambda *_: (layer,) + (0,) * n, pipeline_mode=pl.Buffered(1))


def _in_proj_kernel(x_ref, g_ref, w_ref, b_ref, ws_ref, bs_ref, o16_ref, o32_ref, h_ref):
    @pl.when(pl.program_id(1) == 0)
    def _():
        x = x_ref[...]
        y = x * lax.rsqrt(jnp.mean(x * x, axis=-1, keepdims=True) + EPS)
        hb = (y * g_ref[...]).astype(_BF16)
        h_ref[...] = hb
        o32_ref[...] = _dot(hb, ws_ref[...]) + bs_ref[...]

    o16_ref[...] = (_dot(h_ref[...], w_ref[...]) + b_ref[...]).astype(_BF16)


def _in_proj(x2, g, w16, b16, w32, b32, l):
    n = x2.shape[0]
    tm, tn = 1024, 2048
    return pl.pallas_call(
        _in_proj_kernel,
        out_shape=(jax.ShapeDtypeStruct((n, N16), _BF16), jax.ShapeDtypeStruct((n, N32), _F32)),
        grid=(n // tm, N16 // tn),
        in_specs=[
            pl.BlockSpec((tm, D_MODEL), lambda i, j: (i, 0)),
            pl.BlockSpec((1, D_MODEL), lambda i, j: (0, 0)),
            pl.BlockSpec((None, D_MODEL, tn), lambda i, j: (l, 0, j)),
            pl.BlockSpec((None, 1, tn), lambda i, j: (l, 0, j)),
            pl.BlockSpec((None, D_MODEL, N32), lambda i, j: (l, 0, 0)),
            pl.BlockSpec((None, 1, N32), lambda i, j: (l, 0, 0)),
        ],
        out_specs=(
            pl.BlockSpec((tm, tn), lambda i, j: (i, j)),
            pl.BlockSpec((tm, N32), lambda i, j: (i, 0)),
        ),
        scratch_shapes=[pltpu.VMEM((tm, D_MODEL), _BF16)],
        compiler_params=_params("parallel", "arbitrary"),
        name="in_proj",
    )(x2, g, w16, b16, w32, b32)


def _log_sigmoid(x):
    return jnp.minimum(x, 0.0) - jnp.log(1.0 + jnp.exp(-jnp.abs(x)))


def _cumsum_rows(x):
    n = x.shape[0]
    row = lax.broadcasted_iota(jnp.int32, x.shape, 0)
    k = 1
    while k < n:
        x = x + jnp.where(row >= k, pltpu.roll(x, k, axis=0), 0.0)
        k *= 2
    return x


def _mlstm_kernel(qk_ref, v_ref, o_ref, g32_ref, cw_ref, hn_ref, out_ref,
                  ct_ref, n_ref, m_ref, tail_ref):
    @pl.when(pl.program_id(1) == 0)
    def _():
        ct_ref[...] = jnp.zeros_like(ct_ref)
        n_ref[...] = jnp.zeros_like(n_ref)
        m_ref[...] = jnp.zeros_like(m_ref)
        tail_ref[...] = jnp.zeros_like(tail_ref)

    L = ML
    qk_raw = qk_ref[...].astype(_F32)
    ext = jnp.concatenate([tail_ref[...], qk_raw], axis=0)
    tail_ref[...] = qk_raw[L - SUBLANES:, :]
    cw = cw_ref[...]
    conv = ext[SUBLANES:, :] * cw[CONV_W - 1:CONV_W, :]
    for j in range(1, CONV_W):
        conv = conv + pltpu.roll(ext, j, axis=0)[SUBLANES:, :] * cw[CONV_W - 1 - j:CONV_W - j, :]
    half = 0.5 * conv
    qk = half + half * jnp.tanh(half)

    g32 = g32_ref[...]
    bc = _cumsum_rows(_log_sigmoid(g32))
    g32_t = g32.T
    bc_t = bc.T
    row = lax.broadcasted_iota(jnp.int32, (L, L), 0)
    col = lax.broadcasted_iota(jnp.int32, (L, L), 1)
    causal = col <= row

    for h in range(N_HEADS_M):
        sl = slice(h * DH_M, (h + 1) * DH_M)
        q = qk[:, sl].astype(_BF16)
        k = (qk[:, W_M + h * DH_M:W_M + (h + 1) * DH_M] * (DH_M ** -0.5)).astype(_BF16)
        v = v_ref[:, sl]
        b_c = bc[:, C_MF + h:C_MF + h + 1]
        i_c = g32[:, C_MI + h:C_MI + h + 1]
        b_r = bc_t[C_MF + h:C_MF + h + 1, :]
        i_r = g32_t[C_MI + h:C_MI + h + 1, :]
        m_prev = m_ref[h]

        dlog = jnp.where(causal, b_c - b_r + i_r, NEG)
        inter = b_c + m_prev
        mj = jnp.maximum(inter, jnp.max(dlog, axis=-1, keepdims=True))
        dw = jnp.exp(dlog - mj)
        iw = jnp.exp(inter - mj)
        s = _dot_nt(q, k) * dw
        num = iw * _dot(q, ct_ref[h].astype(_BF16)) + _dot(s.astype(_BF16), v)
        qn = jnp.sum(q.astype(_F32) * n_ref[h], axis=-1, keepdims=True)
        den = iw * qn + jnp.sum(s, axis=-1, keepdims=True)
        hh = num / jnp.maximum(jnp.abs(den), jnp.exp(-mj))

        b_last = b_c[L - 1:L, :]
        gg = b_last - b_c + i_c
        m_new = jnp.maximum(b_last + m_prev, jnp.max(gg, axis=0, keepdims=True))
        decay = jnp.exp(b_last + m_prev - m_new)
        w = jnp.exp(gg - m_new)
        kf = k.astype(_F32)
        wv = (w * v.astype(_F32)).astype(_BF16)
        ct_ref[h] = decay * ct_ref[h] + _dot_tn(k, wv)
        n_ref[h] = decay * n_ref[h] + jnp.sum(w * kf, axis=0, keepdims=True)
        m_ref[h] = m_new

        y = hh * lax.rsqrt(jnp.mean(hh * hh, axis=-1, keepdims=True) + EPS) * hn_ref[:, sl]
        gate = jax.nn.sigmoid(o_ref[:, sl].astype(_F32))
        out_ref[:, sl] = (y * gate).astype(_BF16)


def _mlstm(p16, p32, conv_qk, norm_head, B, S):
    n = B * S
    nc = S // ML
    row = lambda b, c: b * nc + c
    return pl.pallas_call(
        _mlstm_kernel,
        out_shape=jax.ShapeDtypeStruct((n, W_M), _BF16),
        grid=(B, nc),
        in_specs=[
            pl.BlockSpec((ML, 2 * W_M), lambda b, c: (row(b, c), 0)),
            pl.BlockSpec((ML, W_M), lambda b, c: (row(b, c), OFF_MV // W_M)),
            pl.BlockSpec((ML, W_M), lambda b, c: (row(b, c), OFF_MO // W_M)),
            pl.BlockSpec((ML, N32), lambda b, c: (row(b, c), 0)),
            pl.BlockSpec((CONV_W, 2 * W_M), lambda b, c: (0, 0)),
            pl.BlockSpec((1, W_M), lambda b, c: (0, 0)),
        ],
        out_specs=pl.BlockSpec((ML, W_M), lambda b, c: (row(b, c), 0)),
        scratch_shapes=[
            pltpu.VMEM((N_HEADS_M, DH_M, DH_M), _F32),
            pltpu.VMEM((N_HEADS_M, 1, DH_M), _F32),
            pltpu.VMEM((N_HEADS_M, 1, 1), _F32),
            pltpu.VMEM((SUBLANES, 2 * W_M), _F32),
        ],
        compiler_params=_params("parallel", "arbitrary"),
        name="mlstm",
    )(p16, p16, p16, p32, conv_qk, norm_head)


def _rot_rows(x, half):
    return jnp.concatenate([x[half:], x[:half]], axis=0)


def _dsa_prep_kernel(aq_ref, iq_ref, ak_ref, av_ref, g32_ref, gk_ref,
                     cat_ref, sat_ref, cit_ref, sit_ref, ca_ref, sa_ref, ci_ref, si_ref,
                     qat_ref, qit_ref, wit_ref, ka_ref, ki_ref, vt_ref):
    for s in range(TP // TQ):
        rows = slice(s * TQ, (s + 1) * TQ)
        slab = s * HQ
        cat, sat = cat_ref[:, rows], sat_ref[:, rows]
        aq_t = aq_ref[rows, :].astype(_F32).T
        for h in range(N_HEADS_A):
            x = aq_t[h * DH_A:(h + 1) * DH_A]
            qat_ref[:, slab + h * TQ:slab + (h + 1) * TQ] = (
                (x * cat + _rot_rows(x, DH_A // 2) * sat) * QK_SCALE2).astype(_BF16)

        cit, sit = cit_ref[:, rows], sit_ref[:, rows]
        iq_t = iq_ref[rows, :].astype(_F32).T
        for h in range(N_HEADS_IDX):
            x = iq_t[h * DH_IDX:(h + 1) * DH_IDX]
            qit_ref[:, slab + h * TQ:slab + (h + 1) * TQ] = (
                x * cit + _rot_rows(x, DH_IDX // 2) * sit).astype(_BF16)

        w_t = g32_ref[rows, :].T[C_IW:C_IW + N_HEADS_IDX, :] * IDX_W_SCALE
        for h in range(N_HEADS_IDX):
            wit_ref[:, slab + h * TQ:slab + (h + 1) * TQ] = w_t[h:h + 1, :]

    ak = ak_ref[...].astype(_F32)
    ka_ref[...] = (ak * ca_ref[...] + pltpu.roll(ak, DH_A // 2, axis=1) * sa_ref[...]).astype(_BF16)

    ik = g32_ref[:, C_IK:C_IK + DH_IDX]
    xc = ik - jnp.mean(ik, axis=-1, keepdims=True)
    ln = xc * lax.rsqrt(jnp.mean(xc * xc, axis=-1, keepdims=True) + EPS) * gk_ref[...]
    half = DH_IDX // 2
    ln_rot = jnp.concatenate([ln[:, half:], ln[:, :half]], axis=1)
    ki_ref[...] = (ln * ci_ref[...] + ln_rot * si_ref[...]).astype(_BF16)

    vt_ref[:DH_A, :] = av_ref[...].astype(_F32).T.astype(_BF16)
    vt_ref[DH_A:, :] = jnp.ones((VT_ROWS - DH_A, TP), _BF16)


def _dsa_prep(p16, p32, norm_idx_k, tabs, B, S):
    nt = S // TQ
    npt = S // TP
    row = lambda b, t: b * npt + t
    cat, sat, cit, sit, ca, sa, ci, si = tabs
    return pl.pallas_call(
        _dsa_prep_kernel,
        out_shape=(
            jax.ShapeDtypeStruct((B, DH_A, nt * HQ), _BF16),
            jax.ShapeDtypeStruct((B, DH_IDX, nt * HQ), _BF16),
            jax.ShapeDtypeStruct((B, 1, nt * HQ), _F32),
            jax.ShapeDtypeStruct((B, S, DH_A), _BF16),
            jax.ShapeDtypeStruct((B, S, DH_IDX), _BF16),
            jax.ShapeDtypeStruct((B, VT_ROWS, S), _BF16),
        ),
        grid=(B, npt),
        in_specs=[
            pl.BlockSpec((TP, W_A), lambda b, t: (row(b, t), OFF_AQ // W_A)),
            pl.BlockSpec((TP, W_IDX), lambda b, t: (row(b, t), OFF_IQ // W_IDX)),
            pl.BlockSpec((TP, DH_A), lambda b, t: (row(b, t), OFF_AK // DH_A)),
            pl.BlockSpec((TP, DH_A), lambda b, t: (row(b, t), OFF_AV // DH_A)),
            pl.BlockSpec((TP, N32), lambda b, t: (row(b, t), 0)),
            pl.BlockSpec((1, DH_IDX), lambda b, t: (0, 0)),
            pl.BlockSpec((DH_A, TP), lambda b, t: (0, t)),
            pl.BlockSpec((DH_A, TP), lambda b, t: (0, t)),
            pl.BlockSpec((DH_IDX, TP), lambda b, t: (0, t)),
            pl.BlockSpec((DH_IDX, TP), lambda b, t: (0, t)),
            pl.BlockSpec((TP, DH_A), lambda b, t: (t, 0)),
            pl.BlockSpec((TP, DH_A), lambda b, t: (t, 0)),
            pl.BlockSpec((TP, DH_IDX), lambda b, t: (t, 0)),
            pl.BlockSpec((TP, DH_IDX), lambda b, t: (t, 0)),
        ],
        out_specs=(
            pl.BlockSpec((None, DH_A, (TP // TQ) * HQ), lambda b, t: (b, 0, t)),
            pl.BlockSpec((None, DH_IDX, (TP // TQ) * HQ), lambda b, t: (b, 0, t)),
            pl.BlockSpec((None, 1, (TP // TQ) * HQ), lambda b, t: (b, 0, t)),
            pl.BlockSpec((None, TP, DH_A), lambda b, t: (b, t, 0)),
            pl.BlockSpec((None, TP, DH_IDX), lambda b, t: (b, t, 0)),
            pl.BlockSpec((None, VT_ROWS, TP), lambda b, t: (b, 0, t)),
        ),
        compiler_params=_params("parallel", "parallel"),
        name="dsa_prep",
    )(p16, p16, p16, p16, p32, norm_idx_k, cat, sat, cit, sit, ca, sa, ci, si)


def _allsum_sublanes(x):
    for k in (4, 2, 1):
        x = x + pltpu.roll(x, k, axis=0)
    return x


def _key_to_f32(key):
    return pltpu.bitcast(key ^ ((key >> 31) & 0x7FFFFFFF), _F32)


def _dsa_kernel(qit_ref, wit_ref, qat_ref, ki_ref, ka_ref, vt_ref, out_ref,
                sc_ref, hi_ref, lo_ref, lg_ref, acc_ref, m_ref, jmax_ref,
                *, n_sel, idx_bits):
    t = pl.program_id(1)
    nkt = t + 1
    n_keys = nkt * TK
    row_in_tile = lax.broadcasted_iota(jnp.int32, (TK, TQ), 0)
    lane_in_tile = lax.broadcasted_iota(jnp.int32, (TK, TQ), 1)
    adm_diag = (row_in_tile // CHUNK) <= (lane_in_tile // CHUNK)
    heads = [slice(h * TQ, (h + 1) * TQ) for h in range(N_HEADS_A)]
    tile = lambda kt: pl.ds(pl.multiple_of(kt * TK, TK), TK)
    i16 = jnp.int16
    P16 = 2 * SUBLANES

    def score_block(r0, n, diag):
        rows = pl.ds(r0, n)
        r_all = _dot(ki_ref[rows, :], qit_ref[...])
        sc = jnp.zeros((n, TQ), _F32)
        for sl in heads:
            sc = sc + wit_ref[:, sl] * jnp.maximum(r_all[:, sl], 0.0)
        if diag:
            sc = jnp.where(adm_diag, sc, NEG)
        sc = jnp.where(sc == 0.0, 0.0, sc)
        sc_ref[rows, :] = sc
        bits = pltpu.bitcast(sc, jnp.int32)
        key = bits ^ ((bits >> 31) & 0x7FFFFFFF)
        hi_ref[rows, :] = (key >> 16).astype(i16)
        lo_ref[rows, :] = ((key & 0xFFFF) - 32768).astype(i16)

    def score_body(i, carry):
        score_block(pl.multiple_of(i * TKB, TKB), TKB, False)
        return carry

    lax.fori_loop(0, t // 2, score_body, 0)
    pl.when(t % 2 == 1)(lambda: score_block(pl.multiple_of((t - 1) * TK, TK), TK, False))
    score_block(pl.multiple_of(t * TK, TK), TK, True)

    nkb = (nkt + 1) // 2
    block = lambda i: pl.ds(pl.multiple_of(i * TKB, TKB), TKB)

    @pl.when(nkt % 2 == 1)
    def _():
        pad = jnp.full((TK, TQ), -32768, i16)
        hi_ref[tile(nkt), :] = pad
        lo_ref[tile(nkt), :] = pad

    def count16(ref, pred):
        def body(i, accs):
            blk = ref[block(i), :]
            accs = list(accs)
            for g in range(TKB // P16):
                accs[g % 4] = accs[g % 4] + pred(blk[g * P16:(g + 1) * P16])
            return tuple(accs)
        z = jnp.zeros((P16, TQ), i16)
        a = [x.astype(jnp.int32) for x in lax.fori_loop(0, nkb, body, (z, z, z, z))]
        s = (a[0] + a[1]) + (a[2] + a[3])
        return _allsum_sublanes(s[:SUBLANES] + s[SUBLANES:])

    def rows16(x):
        return jnp.concatenate([x, x], axis=0).astype(i16)

    def count_ge16(ref, cand):
        c16 = rows16(cand)
        return count16(ref, lambda b: jnp.where(b >= c16, i16(1), i16(0)))

    def search16(ref, need, c_base):
        def step(i, carry):
            cur, cnt = carry
            cand = cur + lax.shift_left(jnp.int32(1), 15 - i)
            c = count_ge16(ref, cand)
            ok = c >= need
            return jnp.where(ok, cand, cur), jnp.where(ok, c, cnt)
        return lax.fori_loop(0, 16, step, (jnp.full((SUBLANES, TQ), -32768, jnp.int32), c_base))

    zeros8 = jnp.zeros((SUBLANES, TQ), jnp.int32)
    t_hi, c_hi = search16(hi_ref, zeros8 + n_sel, zeros8 + n_keys)
    c_gt = jnp.where(t_hi >= 32767, 0, count_ge16(hi_ref, jnp.minimum(t_hi + 1, 32767)))
    t_hi16 = rows16(t_hi)

    def narrow_body(i, carry):
        rows = block(i)
        hi, lo = hi_ref[rows, :], lo_ref[rows, :]
        lo_ref[rows, :] = jnp.concatenate(
            [jnp.where(hi[g * P16:(g + 1) * P16] == t_hi16, lo[g * P16:(g + 1) * P16], i16(-32768))
             for g in range(TKB // P16)], axis=0)
        return carry

    lax.fori_loop(0, nkb, narrow_body, 0)
    t_lo, c_lo = search16(lo_ref, n_sel - c_gt, c_hi - c_gt)
    cge = c_gt + c_lo
    thr = _key_to_f32((t_hi << 16) | ((t_lo + 32768) & 0xFFFF))
    thr_row = thr[0:1, :]

    def count32(pred):
        def body(kt, accs):
            r0 = pl.multiple_of(kt * TK, TK)
            blk = sc_ref[pl.ds(r0, TK), :]
            accs = list(accs)
            for g in range(TK // SUBLANES):
                accs[g % 4] = accs[g % 4] + pred(blk[g * SUBLANES:(g + 1) * SUBLANES], r0 + g * SUBLANES)
            return tuple(accs)
        a = lax.fori_loop(0, nkt, body, (zeros8, zeros8, zeros8, zeros8))
        return _allsum_sublanes((a[0] + a[1]) + (a[2] + a[3]))

    sub = lax.broadcasted_iota(jnp.int32, (SUBLANES, TQ), 0)
    jmax_ref[...] = jnp.full_like(jmax_ref, 2 ** 30)
    has_ties = jnp.max(cge) > n_sel

    @pl.when(has_ties)
    def _():
        need = n_sel - count32(lambda blk, r: jnp.where(blk > thr, 1, 0))

        def step(i, j):
            cand = j + lax.shift_left(jnp.int32(1), idx_bits - 1 - i)
            c = count32(lambda blk, r: jnp.where(blk == thr, jnp.where(r + sub < cand, 1, 0), 0))
            return jnp.where(c < need, cand, j)

        jmax_ref[...] = lax.fori_loop(0, idx_bits, step, zeros8)

    def bias_tile(kt, general, diag):
        rows = tile(kt)
        blk = sc_ref[rows, :]
        if general:
            krow = kt * TK + row_in_tile
            pick = jnp.where(blk > thr_row, 1,
                             jnp.where(blk == thr_row, jnp.where(krow <= jmax_ref[0:1, :], 1, 0), 0))
            if diag:
                pick = jnp.where(adm_diag, pick, 0)
            sc_ref[rows, :] = jnp.where(pick > 0, 0.0, NEG)
        else:
            sc_ref[rows, :] = jnp.where(blk >= thr_row, 0.0, NEG)

    def bias_loop(general):
        def body(kt, carry):
            bias_tile(kt, general, False)
            return carry
        lax.fori_loop(0, t, body, 0)

    pl.when(has_ties)(lambda: bias_loop(True))
    pl.when(jnp.logical_not(has_ties))(lambda: bias_loop(False))
    bias_tile(t, True, True)

    acc_ref[...] = jnp.zeros_like(acc_ref)
    m_ref[...] = jnp.full_like(m_ref, NEG)

    def logits_stage(kt, buf):
        rows = tile(kt)
        bias = sc_ref[rows, :]
        ka = ka_ref[rows, :]
        tops = []
        for sl in heads:
            x = _dot(ka, qat_ref[:, sl]) + bias
            lg_ref[buf, :, sl] = x
            tops.append(jnp.max(x, axis=0, keepdims=True))
        return jnp.concatenate(tops, axis=0)

    def softmax_stage(kt, buf, top):
        vt = vt_ref[:, tile(kt)]
        m_old = m_ref[...]
        m_new = jnp.maximum(m_old, top)
        m_ref[...] = m_new
        alpha = jnp.exp2(m_old - m_new)
        for h, sl in enumerate(heads):
            p = jnp.exp2(lg_ref[buf, :, sl] - m_new[h:h + 1, :]).astype(_BF16)
            acc_ref[:, sl] = acc_ref[:, sl] * alpha[h:h + 1, :] + _dot(vt, p)

    def attn_body(i, top0):
        top1 = logits_stage(2 * i + 1, 1)
        softmax_stage(2 * i, 0, top0)
        top0 = logits_stage(2 * i + 2, 0)
        softmax_stage(2 * i + 1, 1, top1)
        return top0

    top_last = lax.fori_loop(0, t // 2, attn_body, logits_stage(0, 0))

    @pl.when(t % 2 == 0)
    def _():
        softmax_stage(t, 0, top_last)

    @pl.when(t % 2 == 1)
    def _():
        top1 = logits_stage(t, 1)
        softmax_stage(t - 1, 0, top_last)
        softmax_stage(t, 1, top1)

    for h, sl in enumerate(heads):
        o = acc_ref[:DH_A, sl] / acc_ref[DH_A:DH_A + 1, sl]
        out_ref[:, h * DH_A:(h + 1) * DH_A] = o.T.astype(_BF16)


def _dsa(qat, qit, wit, ka, ki, vt, B, S):
    nt = S // TQ
    n_sel = min(TOPK_MAX, S // 4)
    idx_bits = max(1, int(np.ceil(np.log2(S))))
    kern = functools.partial(_dsa_kernel, n_sel=n_sel, idx_bits=idx_bits)
    return pl.pallas_call(
        kern,
        out_shape=jax.ShapeDtypeStruct((B * S, W_A), _BF16),
        grid=(B, nt),
        in_specs=[
            pl.BlockSpec((None, DH_IDX, HQ), lambda b, t: (b, 0, t)),
            pl.BlockSpec((None, 1, HQ), lambda b, t: (b, 0, t)),
            pl.BlockSpec((None, DH_A, HQ), lambda b, t: (b, 0, t)),
            pl.BlockSpec((None, S, DH_IDX), lambda b, t: (b, 0, 0)),
            pl.BlockSpec((None, S, DH_A), lambda b, t: (b, 0, 0)),
            pl.BlockSpec((None, VT_ROWS, S), lambda b, t: (b, 0, 0)),
        ],
        out_specs=pl.BlockSpec((TQ, W_A), lambda b, t: (b * nt + t, 0)),
        scratch_shapes=[
            pltpu.VMEM((S, TQ), _F32),
            pltpu.VMEM((S, TQ), jnp.int16),
            pltpu.VMEM((S, TQ), jnp.int16),
            pltpu.VMEM((2, TK, HQ), _F32),
            pltpu.VMEM((VT_ROWS, HQ), _F32),
            pltpu.VMEM((N_HEADS_A, TQ), _F32),
            pltpu.VMEM((SUBLANES, TQ), jnp.int32),
        ],
        compiler_params=_params("parallel", "arbitrary"),
        name="dsa",
    )(qit, wit, qat, ki, ka, vt)


def _merge_kernel(hm_ref, ha_ref, gm_ref, ga_ref, x_ref, wbm_ref, wba_ref, wo_ref, g_ref, out_ref):
    ym = _dot(hm_ref[...], wbm_ref[...])
    ya = _dot(ha_ref[...], wba_ref[...])
    y = jax.nn.sigmoid(gm_ref[...].astype(_F32)) * ym + jax.nn.sigmoid(ga_ref[...].astype(_F32)) * ya
    z = _dot(y.astype(_BF16), wo_ref[...])
    zn = z * lax.rsqrt(jnp.mean(z * z, axis=-1, keepdims=True) + EPS) * g_ref[...]
    out_ref[...] = x_ref[...] + zn


def _merge(hm, ha, p16, x2, wbm, wba, wo, g, l):
    n = x2.shape[0]
    tm = 512
    rows = lambda c: pl.BlockSpec((tm, D_MODEL), lambda i: (i, c))
    return pl.pallas_call(
        _merge_kernel,
        out_shape=jax.ShapeDtypeStruct((n, D_MODEL), _F32),
        grid=(n // tm,),
        in_specs=[rows(0), rows(0), rows(OFF_GM // D_MODEL), rows(OFF_GA // D_MODEL), rows(0),
                  _resident((W_M, D_MODEL), l), _resident((W_A, D_MODEL), l), _resident((D_MODEL, D_MODEL), l),
                  _resident((1, D_MODEL))],
        out_specs=rows(0),
        compiler_params=_params("parallel"),
        name="merge",
    )(hm, ha, p16, p16, x2, wbm, wba, wo, g)


def _ffn_kernel(x_ref, gpre_ref, wg_ref, wu_ref, wd_ref, gpost_ref, out_ref):
    x = x_ref[...]
    f = (x * lax.rsqrt(jnp.mean(x * x, axis=-1, keepdims=True) + EPS) * gpre_ref[...]).astype(_BF16)
    a = _dot(f, wg_ref[...])
    u = _dot(f, wu_ref[...])
    act = (a * jax.nn.sigmoid(a) * u).astype(_BF16)
    z = _dot(act, wd_ref[...])
    out_ref[...] = x + z * lax.rsqrt(jnp.mean(z * z, axis=-1, keepdims=True) + EPS) * gpost_ref[...]


def _ffn(x2, gpre, wg, wu, wd, gpost, l):
    n = x2.shape[0]
    tm = 512
    rows = pl.BlockSpec((tm, D_MODEL), lambda i: (i, 0))
    return pl.pallas_call(
        _ffn_kernel,
        out_shape=jax.ShapeDtypeStruct((n, D_MODEL), _F32),
        grid=(n // tm,),
        in_specs=[rows, _resident((1, D_MODEL)), _resident((D_MODEL, D_FF), l), _resident((D_MODEL, D_FF), l),
                  _resident((D_FF, D_MODEL), l), _resident((1, D_MODEL))],
        out_specs=rows,
        compiler_params=_params("parallel"),
        name="ffn",
    )(x2, gpre, wg, wu, wd, gpost)


def _pack_in_proj(w, b):
    o = np.cumsum((0, W_M, W_M, W_M, W_M, N_HEADS_M, N_HEADS_M, W_A, DH_A, DH_A,
                   W_IDX, N_HEADS_IDX, DH_IDX, D_MODEL, D_MODEL))
    mq, mk, mv, mo, mi, mf, aq, ak, av, iq, iw, ik, gm, ga = [slice(o[i], o[i + 1]) for i in range(14)]
    order16 = (mq, mk, mv, mo, aq, gm, ga, iq, ak, av)
    order32 = (ik, iw, mi, mf)

    def cat(arr, order, width):
        parts = [arr[..., s] for s in order]
        used = sum(p.shape[-1] for p in parts)
        parts.append(jnp.zeros(arr.shape[:-1] + (width - used,), arr.dtype))
        return jnp.concatenate(parts, axis=-1)

    return (cat(w, order16, N16).astype(_BF16), cat(b, order16, N16)[..., None, :],
            cat(w, order32, N32).astype(_BF16), cat(b, order32, N32)[..., None, :])


def _rope_tables(S):
    def tab(dim):
        inv = ROPE_THETA ** (-jnp.arange(dim // 2, dtype=_F32) / (dim // 2))
        ang = jnp.arange(S, dtype=_F32)[:, None] * inv[None, :]
        c, s = jnp.cos(ang), jnp.sin(ang)
        return jnp.concatenate([c, c], axis=-1), jnp.concatenate([-s, s], axis=-1)

    ca, sa = tab(DH_A)
    ci, si = tab(DH_IDX)
    return (ca.T, sa.T, ci.T, si.T, ca, sa, ci, si)


def kernel(x, norm_mix_pre, norm_mix_post, norm_ffn_pre, norm_ffn_post, w_in, b_in, conv_qk,
           norm_mlstm_head, norm_idx_k, w_branch_mlstm, w_branch_attn, w_out,
           w_ffn_gate, w_ffn_up, w_ffn_down):
    B, S, _ = x.shape
    assert S % ML == 0 and S % TKB == 0 and S % TP == 0 and TQ == TK and (B * S) % 1024 == 0
    depth = w_in.shape[0]
    tabs = _rope_tables(S)
    x2 = x.reshape(B * S, D_MODEL)
    w16, b16, w32, b32 = _pack_in_proj(w_in, b_in)
    wbm, wba, wo = (w.astype(_BF16) for w in (w_branch_mlstm, w_branch_attn, w_out))
    wg, wu, wd = (w.astype(_BF16) for w in (w_ffn_gate, w_ffn_up, w_ffn_down))
    for l in range(depth):
        p16, p32 = _in_proj(x2, norm_mix_pre[l][None, :], w16, b16, w32, b32, l)
        hm = _mlstm(p16, p32, conv_qk[l], norm_mlstm_head[l][None, :], B, S)
        qat, qit, wit, ka, ki, vt = _dsa_prep(p16, p32, norm_idx_k[l][None, :], tabs, B, S)
        ha = _dsa(qat, qit, wit, ka, ki, vt, B, S)
        x2 = _merge(hm, ha, p16, x2, wbm, wba, wo, norm_mix_post[l][None, :], l)
        x2 = _ffn(x2, norm_ffn_pre[l][None, :], wg, wu, wd, norm_ffn_post[l][None, :], l)
    return x2.reshape(B, S, D_MODEL)
```
